```python
import math
import jax, jax.numpy as jnp
from jax import lax
import numpy as np

D_MODEL = 1024
BATCH = 8
SEQ = 2048
DEPTH = 4
DEC_BATCH = 8
DEC_SEQ = 32
PAST_LEN = 1024

CHUNK = 64
Q_BLOCK = 128
N_MIXERS = 3
N_A = (DEPTH + 2) // 3
N_B = (DEPTH + 1) // 3
N_C = DEPTH // 3
EPS = 1e-6

A_PROJ = 2
A_INNER = A_PROJ * D_MODEL
A_HEADS = 4
A_DH = A_INNER // A_HEADS
A_CONV = 4

B_HEADS = 16
B_DH = D_MODEL // B_HEADS
SB_SCALE = B_DH ** -0.5

C_HEADS = 16
C_Q_RANK = 384
C_KV_RANK = 256
C_NOPE = 64
C_ROPE = 32
C_VDIM = 64
C_SCALE = (C_NOPE + C_ROPE) ** -0.5
ROPE_THETA = 10000.0

D_FF = 4 * D_MODEL

kernel_name = 'hybrid_mlstm_stickbreak_mla_stream_step'


def _rmsnorm(x, g):
    xf = x.astype(jnp.float32)
    y = xf * lax.rsqrt(jnp.mean(xf * xf, axis=-1, keepdims=True) + EPS)
    return (y * g.astype(jnp.float32)).astype(x.dtype)


def _layernorm_heads(h, g):
    mu = jnp.mean(h, axis=-1, keepdims=True)
    var = jnp.mean(jnp.square(h - mu), axis=-1, keepdims=True)
    return (h - mu) * lax.rsqrt(var + EPS) * g


def _rope(x, offset):
    t_len, r = x.shape[1], x.shape[-1]
    half = r // 2
    inv = ROPE_THETA ** (-jnp.arange(half, dtype=jnp.float32) / half)
    ang = (jnp.arange(t_len, dtype=jnp.float32) + offset)[:, None] * inv[None, :]
    ang = ang.reshape((t_len,) + (1,) * (x.ndim - 3) + (half,))
    cos, sin = jnp.cos(ang), jnp.sin(ang)
    xf = x.astype(jnp.float32)
    x1, x2 = xf[..., :half], xf[..., half:]
    return jnp.concatenate([x1 * cos - x2 * sin, x1 * sin + x2 * cos], axis=-1).astype(x.dtype)


def _q_blocks(n_q):
    return [(a, min(a + Q_BLOCK, n_q)) for a in range(0, n_q, Q_BLOCK)]


def _mlstm_chunk(carry, inp):
    C, n, m = carry
    q, k, v, li, lf = inp
    L = q.shape[2]
    b = jnp.cumsum(lf, axis=-1)
    causal = jnp.tril(jnp.ones((L, L), dtype=bool))
    dmat = jnp.where(causal, b[..., :, None] - b[..., None, :] + li[..., None, :], -jnp.inf)
    inter = b + m[..., None]
    m_t = jnp.maximum(inter, jnp.max(dmat, axis=-1))
    w_inter = jnp.exp(inter - m_t)
    s = jnp.einsum('bhtd,bhsd->bhts', q, k) * jnp.exp(dmat - m_t[..., None])
    num = w_inter[..., None] * jnp.einsum('bhtd,bhde->bhte', q, C) + jnp.einsum('bhts,bhse->bhte', s, v)
    nq = w_inter * jnp.einsum('bhtd,bhd->bht', q, n) + jnp.sum(s, axis=-1)
    h = num / jnp.maximum(jnp.abs(nq), jnp.exp(-m_t))[..., None]
    m_new = m_t[..., -1]
    g_state = jnp.exp(b[..., -1] + m - m_new)
    g_tok = jnp.exp(b[..., -1:] - b + li - m_new[..., None])
    C_new = g_state[..., None, None] * C + jnp.einsum('bhs,bhsd,bhse->bhde', g_tok, k, v)
    n_new = g_state[..., None] * n + jnp.einsum('bhs,bhsd->bhd', g_tok, k)
    return (C_new, n_new, m_new), h


def _mlstm_mixer(h, C0, n0, m0, conv0, w_up, conv_w, conv_b, w_q, w_k, w_v, w_gate, b_i, b_f,
                 g_head, skip, w_down):
    f32 = jnp.float32
    bsz, t_len, _ = h.shape
    up = h @ w_up
    xm, z = up[..., :A_INNER], up[..., A_INNER:]
    xpad = jnp.concatenate([conv0.astype(xm.dtype), xm], axis=1)
    xc = conv_b
    for j in range(A_CONV):
        xc = xc + xpad[:, j:j + t_len] * conv_w[j]
    xc = jax.nn.silu(xc)
    xc_h = xc.reshape(bsz, t_len, A_HEADS, A_DH)
    xm_h = xm.reshape(bsz, t_len, A_HEADS, A_DH)
    q = jnp.einsum('bthd,hde->bthe', xc_h, w_q)
    k = jnp.einsum('bthd,hde->bthe', xc_h, w_k) * (A_DH ** -0.5)
    v = jnp.einsum('bthd,hde->bthe', xm_h, w_v)
    gin = jnp.concatenate([q.reshape(bsz, t_len, A_INNER), k.reshape(bsz, t_len, A_INNER),
                           v.reshape(bsz, t_len, A_INNER)], axis=-1)
    g = (gin @ w_gate).astype(f32)
    li = jnp.swapaxes(g[..., :A_HEADS] + b_i, 1, 2)
    lf = jnp.swapaxes(jax.nn.log_sigmoid(g[..., A_HEADS:] + b_f), 1, 2)
    L = CHUNK if t_len % CHUNK == 0 else t_len
    nc = t_len // L

    def to_chunks(a):
        a = a.reshape(a.shape[:2] + (nc, L) + a.shape[3:])
        return jnp.moveaxis(a, 2, 0)

    qh, kh, vh = (jnp.swapaxes(a, 1, 2).astype(f32) for a in (q, k, v))
    carry0 = (C0.astype(f32), n0.astype(f32), m0.astype(f32))
    (C, n, m), hs = lax.scan(_mlstm_chunk, carry0,
                             (to_chunks(qh), to_chunks(kh), to_chunks(vh), to_chunks(li), to_chunks(lf)))
    hs = jnp.moveaxis(hs, 0, 2).reshape(bsz, A_HEADS, t_len, A_DH).transpose(0, 2, 1, 3)
    hn = _layernorm_heads(hs, g_head.reshape(A_HEADS, A_DH).astype(f32)).reshape(bsz, t_len, A_INNER)
    out = (hn.astype(h.dtype) + skip * xc) * jax.nn.sigmoid(z)
    return out @ w_down, (C, n, m, xpad[:, -(A_CONV - 1):])


def _stick_breaking(q, k, v, q_off):
    f32 = jnp.float32
    outs = []
    for a, e in _q_blocks(q.shape[1]):
        nk = max(1, min(k.shape[1], q_off + e - 1))
        qb, kb, vb = q[:, a:e].astype(f32), k[:, :nk].astype(f32), v[:, :nk].astype(f32)
        z = jnp.einsum('bqhd,bkhd->bhqk', qb, kb) * SB_SCALE
        tq = q_off + jnp.arange(a, e)
        before = jnp.arange(nk)[None, :] < tq[:, None]
        log_fail = jnp.where(before, jax.nn.log_sigmoid(-z), 0.0)
        rest = lax.cumsum(log_fail, axis=3, reverse=True) - log_fail
        att = jnp.where(before, jnp.exp(jax.nn.log_sigmoid(z) + rest), 0.0)
        outs.append(jnp.einsum('bhqk,bkhd->bqhd', att, vb))
    return jnp.concatenate(outs, axis=1).astype(q.dtype)


def _sb_mixer(h, k_past, v_past, q_off, w_qkv, w_o):
    bsz, t_len, _ = h.shape
    qkv = (h @ w_qkv).reshape(bsz, t_len, 3, B_HEADS, B_DH)
    q, k, v = qkv[:, :, 0], qkv[:, :, 1], qkv[:, :, 2]
    k_all = k if k_past is None else jnp.concatenate([k_past.astype(k.dtype), k], axis=1)
    v_all = v if v_past is None else jnp.concatenate([v_past.astype(v.dtype), v], axis=1)
    o = _stick_breaking(q, k_all, v_all, q_off)
    return o.reshape(bsz, t_len, D_MODEL) @ w_o, (k, v)


def _mla_attention(q_nope, q_rope, k_nope, v, k_rope, q_off):
    f32 = jnp.float32
    tk = k_nope.shape[1]
    outs = []
    for a, e in _q_blocks(q_nope.shape[1]):
        nk = min(tk, ((q_off + e - 1) // CHUNK + 1) * CHUNK)
        s = (jnp.einsum('bqhd,bkhd->bhqk', q_nope[:, a:e].astype(f32), k_nope[:, :nk].astype(f32))
             + jnp.einsum('bqhr,bkr->bhqk', q_rope[:, a:e].astype(f32), k_rope[:, :nk].astype(f32))) * C_SCALE
        tq = q_off + jnp.arange(a, e)
        vis = (jnp.arange(nk)[None, :] // CHUNK) <= (tq[:, None] // CHUNK)
        p = jax.nn.softmax(jnp.where(vis, s, -jnp.inf), axis=-1)
        outs.append(jnp.einsum('bhqk,bkhd->bqhd', p, v[:, :nk].astype(f32)))
    return jnp.concatenate(outs, axis=1).astype(q_nope.dtype)


def _mla_mixer(h, ckv_past, kr_past, q_off, w_dq, g_q, w_uq, w_dkv, g_kv, w_ukv, w_o):
    bsz, t_len, _ = h.shape
    cq = _rmsnorm(h @ w_dq, g_q)
    q = (cq @ w_uq).reshape(bsz, t_len, C_HEADS, C_NOPE + C_ROPE)
    q_nope, q_rope = q[..., :C_NOPE], _rope(q[..., C_NOPE:], q_off)
    kv_a = h @ w_dkv
    ckv = _rmsnorm(kv_a[..., :C_KV_RANK], g_kv)
    kr = _rope(kv_a[..., C_KV_RANK:], q_off)
    ckv_all = ckv if ckv_past is None else jnp.concatenate([ckv_past.astype(ckv.dtype), ckv], axis=1)
    kr_all = kr if kr_past is None else jnp.concatenate([kr_past.astype(kr.dtype), kr], axis=1)
    kv = jnp.einsum('bsc,chd->bshd', ckv_all, w_ukv)
    o = _mla_attention(q_nope, q_rope, kv[..., :C_NOPE], kv[..., C_NOPE:], kr_all, q_off)
    return o.reshape(bsz, t_len, C_HEADS * C_VDIM) @ w_o, (ckv, kr)


def _sq_relu_mlp(h, w1, w2):
    return jnp.square(jax.nn.relu(h @ w1)) @ w2


def _trunk(x, q_off, st_C, st_n, st_m, st_conv, sb_k, sb_v, mla_ckv, mla_kr,
           norm_mix, norm_mlp, norm_final,
           a_w_up, a_conv_w, a_conv_b, a_w_q, a_w_k, a_w_v, a_w_gate, a_b_i, a_b_f, a_g_head, a_skip, a_w_down,
           b_w_qkv, b_w_o,
           c_w_dq, c_g_q, c_w_uq, c_w_dkv, c_g_kv, c_w_ukv, c_w_o,
           w_ff1, w_ff2):
    new_C, new_n, new_m, new_conv = [], [], [], []
    new_k, new_v, new_ckv, new_kr = [], [], [], []
    for i in range(DEPTH):
        j = i // N_MIXERS
        h = _rmsnorm(x, norm_mix[i])
        if i % N_MIXERS == 0:
            y, (C, n, m, cb) = _mlstm_mixer(h, st_C[j], st_n[j], st_m[j], st_conv[j], a_w_up[j], a_conv_w[j],
                                           a_conv_b[j], a_w_q[j], a_w_k[j], a_w_v[j], a_w_gate[j], a_b_i[j],
                                           a_b_f[j], a_g_head[j], a_skip[j], a_w_down[j])
            new_C.append(C)
            new_n.append(n)
            new_m.append(m)
            new_conv.append(cb)
        elif i % N_MIXERS == 1:
            y, (k, v) = _sb_mixer(h, None if sb_k is None else sb_k[j], None if sb_v is None else sb_v[j],
                                  q_off, b_w_qkv[j], b_w_o[j])
            new_k.append(k)
            new_v.append(v)
        else:
            y, (ckv, kr) = _mla_mixer(h, None if mla_ckv is None else mla_ckv[j],
                                      None if mla_kr is None else mla_kr[j], q_off, c_w_dq[j], c_g_q[j],
                                      c_w_uq[j], c_w_dkv[j], c_g_kv[j], c_w_ukv[j], c_w_o[j])
            new_ckv.append(ckv)
            new_kr.append(kr)
        x = x + y
        x = x + _sq_relu_mlp(_rmsnorm(x, norm_mlp[i]), w_ff1[i], w_ff2[i])
    return (_rmsnorm(x, norm_final), jnp.stack(new_C), jnp.stack(new_n), jnp.stack(new_m), jnp.stack(new_conv),
            jnp.stack(new_k), jnp.stack(new_v), jnp.stack(new_ckv), jnp.stack(new_kr))


def setup_inputs(seed: int = 0) -> dict:
    key = jax.random.key(seed)
    ks = iter(jax.random.split(key, 48))
    f32 = jnp.float32

    def nrm(shape, scale=1.0):
        return jax.random.normal(next(ks), shape, f32) * scale

    def gain(shape):
        return 1.0 + 0.05 * nrm(shape)

    D = D_MODEL
    return {
        'x_prompt': nrm((BATCH, SEQ, D)),
        'x_sample': nrm((DEC_BATCH, DEC_SEQ, D)),
        'state_mlstm_C': nrm((N_A, DEC_BATCH, A_HEADS, A_DH, A_DH), 0.05),
        'state_mlstm_n': nrm((N_A, DEC_BATCH, A_HEADS, A_DH), 0.05),
        'state_mlstm_m': nrm((N_A, DEC_BATCH, A_HEADS)),
        'state_mlstm_conv': nrm((N_A, DEC_BATCH, A_CONV - 1, A_INNER)),
        'cache_sb_k': nrm((N_B, DEC_BATCH, PAST_LEN, B_HEADS, B_DH)),
        'cache_sb_v': nrm((N_B, DEC_BATCH, PAST_LEN, B_HEADS, B_DH)),
        'cache_mla_ckv': nrm((N_C, DEC_BATCH, PAST_LEN, C_KV_RANK)),
        'cache_mla_krope': nrm((N_C, DEC_BATCH, PAST_LEN, C_ROPE)),
        'norm_mix': gain((DEPTH, D)),
        'norm_mlp': gain((DEPTH, D)),
        'norm_final': gain((D,)),
        'a_w_up': nrm((N_A, D, 2 * A_INNER), D ** -0.5),
        'a_conv_w': nrm((N_A, A_CONV, A_INNER), A_CONV ** -0.5),
        'a_conv_b': nrm((N_A, A_INNER), 0.02),
        'a_w_q': nrm((N_A, A_HEADS, A_DH, A_DH), A_DH ** -0.5),
        'a_w_k': nrm((N_A, A_HEADS, A_DH, A_DH), A_DH ** -0.5),
        'a_w_v': nrm((N_A, A_HEADS, A_DH, A_DH), A_DH ** -0.5),
        'a_w_gate': nrm((N_A, 3 * A_INNER, 2 * A_HEADS), (3 * A_INNER) ** -0.5),
        'a_b_i': nrm((N_A, A_HEADS), 0.1),
        'a_b_f': 3.0 + 3.0 * jax.random.uniform(next(ks), (N_A, A_HEADS), f32),
        'a_g_head': gain((N_A, A_INNER)),
        'a_skip': gain((N_A, A_INNER)),
        'a_w_down': nrm((N_A, A_INNER, D), A_INNER ** -0.5),
        'b_w_qkv': nrm((N_B, D, 3 * D), D ** -0.5),
        'b_w_o': nrm((N_B, D, D), D ** -0.5),
        'c_w_dq': nrm((N_C, D, C_Q_RANK), D ** -0.5),
        'c_g_q': gain((N_C, C_Q_RANK)),
        'c_w_uq': nrm((N_C, C_Q_RANK, C_HEADS * (C_NOPE + C_ROPE)), C_Q_RANK ** -0.5),
        'c_w_dkv': nrm((N_C, D, C_KV_RANK + C_ROPE), D ** -0.5),
        'c_g_kv': gain((N_C, C_KV_RANK)),
        'c_w_ukv': nrm((N_C, C_KV_RANK, C_HEADS, C_NOPE + C_VDIM), C_KV_RANK ** -0.5),
        'c_w_o': nrm((N_C, C_HEADS * C_VDIM, D), (C_HEADS * C_VDIM) ** -0.5),
        'w_ff1': nrm((DEPTH, D, D_FF), D ** -0.5),
        'w_ff2': nrm((DEPTH, D_FF, D), D_FF ** -0.5),
    }


def reference(x_prompt, x_sample, state_mlstm_C, state_mlstm_n, state_mlstm_m, state_mlstm_conv,
              cache_sb_k, cache_sb_v, cache_mla_ckv, cache_mla_krope,
              norm_mix, norm_mlp, norm_final,
              a_w_up, a_conv_w, a_conv_b, a_w_q, a_w_k, a_w_v, a_w_gate, a_b_i, a_b_f, a_g_head, a_skip, a_w_down,
              b_w_qkv, b_w_o,
              c_w_dq, c_g_q, c_w_uq, c_w_dkv, c_g_kv, c_w_ukv, c_w_o,
              w_ff1, w_ff2):
    shared = (norm_mix, norm_mlp, norm_final,
              a_w_up, a_conv_w, a_conv_b, a_w_q, a_w_k, a_w_v, a_w_gate, a_b_i, a_b_f, a_g_head, a_skip, a_w_down,
              b_w_qkv, b_w_o,
              c_w_dq, c_g_q, c_w_uq, c_w_dkv, c_g_kv, c_w_ukv, c_w_o,
              w_ff1, w_ff2)
    bp = x_prompt.shape[0]
    z_C = jnp.zeros((N_A, bp, A_HEADS, A_DH, A_DH), jnp.float32)
    z_n = jnp.zeros((N_A, bp, A_HEADS, A_DH), jnp.float32)
    z_m = jnp.zeros((N_A, bp, A_HEADS), jnp.float32)
    z_conv = jnp.zeros((N_A, bp, A_CONV - 1, A_INNER), x_prompt.dtype)
    (y_prompt, p_C, p_n, p_m, p_conv, p_sb_k, p_sb_v, p_ckv, p_krope) = _trunk(
        x_prompt, 0, z_C, z_n, z_m, z_conv, None, None, None, None, *shared)
    past_len = cache_sb_k.shape[2]
    (y_sample, s_C, s_n, s_m, s_conv, s_sb_k, s_sb_v, s_ckv, s_krope) = _trunk(
        x_sample, past_len, state_mlstm_C, state_mlstm_n, state_mlstm_m, state_mlstm_conv,
        cache_sb_k, cache_sb_v, cache_mla_ckv, cache_mla_krope, *shared)
    return (y_prompt, y_sample, p_C, p_n, p_m, p_conv, p_sb_k, p_sb_v, p_ckv, p_krope,
            s_C, s_n, s_m, s_conv, s_sb_k, s_sb_v, s_ckv, s_krope)
```

```python
import functools

import jax
import jax.numpy as jnp
from jax import lax
from jax.experimental import pallas as pl
from jax.experimental.pallas import tpu as pltpu

F32 = jnp.float32
BF16 = jnp.bfloat16

EPS = 1e-6
DEPTH = 4
CHUNK = 64
A_HEADS = 4
A_DH = 512
A_INNER = A_HEADS * A_DH
A_CONV = 4
B_HEADS = 16
B_DH = 64
C_HEADS = 16
C_NOPE = 64
C_ROPE = 32
C_VDIM = 64
C_KV_RANK = 256
C_SCALE = (C_NOPE + C_ROPE) ** -0.5
ROPE_THETA = 10000.0

LANES = 128
HALF = LANES // 2
NEG_BIG = -1e30
VMEM_LIMIT_BYTES = 56 * 1024 * 1024


def _cparams(*sem):
    return pltpu.CompilerParams(dimension_semantics=sem, vmem_limit_bytes=VMEM_LIMIT_BYTES)


def _tile(n, pref):
    if n <= pref:
        return n
    t = pref
    while n % t:
        t //= 2
    return t


def _const_spec(shape):
    nd = len(shape)
    return pl.BlockSpec(shape, lambda *_: (0,) * nd)


def _rms(x, g):
    return x * lax.rsqrt(jnp.mean(x * x, axis=-1, keepdims=True) + EPS) * g


def _log_sigmoid(x):
    return jnp.minimum(x, 0.0) - jnp.log1p(jnp.exp(-jnp.abs(x)))


def _proj_kernel(*refs, n_outs, has_norm, has_res):
    refs = list(refs)
    x_ref = refs.pop(0)
    g_ref = refs.pop(0) if has_norm else None
    w_refs = [refs.pop(0) for _ in n_outs]
    r_ref = refs.pop(0) if has_res else None
    if has_norm:
        h = _rms(x_ref[...].astype(F32), g_ref[...]).astype(BF16)
    else:
        h = x_ref[...].astype(BF16)
    for idx, w_ref in enumerate(w_refs):
        o_refs = [refs.pop(0) for _ in range(n_outs[idx])]
        n = w_ref.shape[1]
        c = n if n <= 512 else 512
        for n0 in range(0, n, c):
            y = jnp.dot(h, w_ref[:, n0:n0 + c], preferred_element_type=F32)
            if has_res and idx == 0:
                y = y + r_ref[:, n0:n0 + c]
            for o_ref in o_refs:
                o_ref[:, n0:n0 + c] = y.astype(o_ref.dtype)


def _proj(x, gain, weights, out_dtypes, residual=None, tm_pref=512):
    m, k = x.shape
    tm = _tile(m, tm_pref)
    in_specs = [pl.BlockSpec((tm, k), lambda i: (i, 0))]
    args = [x]
    if gain is not None:
        in_specs.append(_const_spec((1, k)))
        args.append(gain.reshape(1, k).astype(F32))
    for w in weights:
        in_specs.append(_const_spec(w.shape))
        args.append(w)
    if residual is not None:
        in_specs.append(pl.BlockSpec((tm, residual.shape[1]), lambda i: (i, 0)))
        args.append(residual)
    out_dtypes = [dt if isinstance(dt, tuple) else (dt,) for dt in out_dtypes]
    out_shape, out_specs = [], []
    for w, dts in zip(weights, out_dtypes):
        for dt in dts:
            out_shape.append(jax.ShapeDtypeStruct((m, w.shape[1]), dt))
            out_specs.append(pl.BlockSpec((tm, w.shape[1]), lambda i: (i, 0)))
    kern = functools.partial(_proj_kernel, n_outs=tuple(len(d) for d in out_dtypes),
                             has_norm=gain is not None, has_res=residual is not None)
    return pl.pallas_call(kern, grid=(m // tm,), in_specs=in_specs, out_specs=out_specs,
                          out_shape=out_shape, compiler_params=_cparams("parallel"))(*args)


def _mlp_kernel(*refs, final):
    if final:
        x_ref, g_ref, w1_ref, w2_ref, gf_ref, o_ref, h_ref, acc_ref = refs
    else:
        x_ref, g_ref, w1_ref, w2_ref, o_ref, h_ref, acc_ref = refs
    f = pl.program_id(1)

    @pl.when(f == 0)
    def _():
        x = x_ref[...]
        h_ref[...] = _rms(x, g_ref[...]).astype(BF16)
        acc_ref[...] = x

    a = jnp.dot(h_ref[...], w1_ref[...], preferred_element_type=F32)
    a = jnp.maximum(a, 0.0)
    a = a * a
    acc_ref[...] += jnp.dot(a.astype(BF16), w2_ref[...], preferred_element_type=F32)

    @pl.when(f == pl.num_programs(1) - 1)
    def _():
        y = acc_ref[...]
        if final:
            y = _rms(y, gf_ref[...])
        o_ref[...] = y


def _mlp(x, gain, w1, w2, final_gain=None, tm_pref=512, tf_pref=1024):
    m, d = x.shape
    dff = w1.shape[1]
    tm = _tile(m, tm_pref)
    tf = _tile(dff, tf_pref)
    in_specs = [pl.BlockSpec((tm, d), lambda i, f: (i, 0)),
                _const_spec((1, d)),
                pl.BlockSpec((d, tf), lambda i, f: (0, f)),
                pl.BlockSpec((tf, d), lambda i, f: (f, 0))]
    args = [x, gain.reshape(1, d), w1, w2]
    if final_gain is not None:
        in_specs.append(_const_spec((1, d)))
        args.append(final_gain.reshape(1, d))
    return pl.pallas_call(
        functools.partial(_mlp_kernel, final=final_gain is not None),
        grid=(m // tm, dff // tf), in_specs=in_specs,
        out_specs=pl.BlockSpec((tm, d), lambda i, f: (i, 0)),
        out_shape=jax.ShapeDtypeStruct((m, d), F32),
        scratch_shapes=[pltpu.VMEM((tm, d), BF16), pltpu.VMEM((tm, d), F32)],
        compiler_params=_cparams("parallel", "arbitrary"))(*args)


def _mlstm_front_kernel(xm_ref, halo_ref, conv0_ref, cw_ref, cb_ref, wq_ref, wk_ref, wv_ref, wg_ref,
                        gb_ref, q_ref, k_ref, v_ref, xc_ref, gcol_ref, grow_ref, xp_ref, *, tt):
    t = pl.program_id(1)
    xm = xm_ref[0]
    prev = jnp.where(t == 0, conv0_ref[0], halo_ref[0])
    xp_ref[0:8, :] = prev
    xp_ref[8:8 + tt, :] = xm
    xc = cb_ref[...] + xm * cw_ref[A_CONV - 1:A_CONV, :]
    for j in range(A_CONV - 1):
        xc = xc + xp_ref[5 + j:5 + j + tt, :] * cw_ref[j:j + 1, :]
    xc = xc / (1.0 + jnp.exp(-xc))
    xc_ref[0] = xc.astype(BF16)
    g = jnp.zeros((tt, LANES), F32)
    for h in range(A_HEADS):
        sl = slice(h * A_DH, (h + 1) * A_DH)
        xch = xc[:, sl].astype(BF16)
        xmh = xm[:, sl].astype(BF16)
        qh = jnp.dot(xch, wq_ref[h], preferred_element_type=F32).astype(BF16)
        kh = (jnp.dot(xch, wk_ref[h], preferred_element_type=F32) * (A_DH ** -0.5)).astype(BF16)
        vh = jnp.dot(xmh, wv_ref[h], preferred_element_type=F32).astype(BF16)
        q_ref[0, :, sl] = qh
        k_ref[0, :, sl] = kh
        v_ref[0, :, sl] = vh
        g = g + jnp.dot(qh, wg_ref[0, h], preferred_element_type=F32)
        g = g + jnp.dot(kh, wg_ref[1, h], preferred_element_type=F32)
        g = g + jnp.dot(vh, wg_ref[2, h], preferred_element_type=F32)
    g = g + gb_ref[...]
    lane = lax.broadcasted_iota(jnp.int32, (tt, LANES), 1)
    g = jnp.where(lane < A_HEADS, g, _log_sigmoid(g))
    gcol_ref[0] = g
    sel = (lax.broadcasted_iota(jnp.int32, (8, LANES), 0) ==
           lax.broadcasted_iota(jnp.int32, (8, LANES), 1)).astype(BF16)
    grow = jnp.zeros((8, tt), F32)
    rem = g
    for _ in range(3):
        part = rem.astype(BF16)
        grow = grow + lax.dot_general(sel, part, (((1,), (1,)), ((), ())), preferred_element_type=F32)
        rem = rem - part.astype(F32)
    grow_ref[0] = grow


def _mlstm_front(up, conv0p, cw, cb, wq, wk, wv, wg, gb, tt_pref=256):
    bsz, t_len, _ = up.shape
    tt = _tile(t_len, tt_pref)
    nblk8 = tt // 8
    act = lambda dt: jax.ShapeDtypeStruct((bsz, t_len, A_INNER), dt)
    act_spec = pl.BlockSpec((1, tt, A_INNER), lambda b, t: (b, t, 0))
    return pl.pallas_call(
        functools.partial(_mlstm_front_kernel, tt=tt),
        grid=(bsz, t_len // tt),
        in_specs=[act_spec,
                  pl.BlockSpec((1, 8, A_INNER), lambda b, t: (b, jnp.maximum(t * nblk8 - 1, 0), 0)),
                  pl.BlockSpec((1, 8, A_INNER), lambda b, t: (b, 0, 0)),
                  _const_spec(cw.shape), _const_spec(cb.shape), _const_spec(wq.shape),
                  _const_spec(wk.shape), _const_spec(wv.shape), _const_spec(wg.shape),
                  _const_spec(gb.shape)],
        out_specs=[act_spec, act_spec, act_spec, act_spec,
                   pl.BlockSpec((1, tt, LANES), lambda b, t: (b, t, 0)),
                   pl.BlockSpec((1, 8, tt), lambda b, t: (b, 0, t))],
        out_shape=[act(BF16), act(BF16), act(BF16), act(BF16),
                   jax.ShapeDtypeStruct((bsz, t_len, LANES), F32),
                   jax.ShapeDtypeStruct((bsz, 8, t_len), F32)],
        scratch_shapes=[pltpu.VMEM((tt + 8, A_INNER), F32)],
        compiler_params=_cparams("parallel", "arbitrary"))(up, up, conv0p, cw, cb, wq, wk, wv, wg, gb)


def _mlstm_rec_kernel(q_ref, k_ref, v_ref, gcol_ref, grow_ref, c0_ref, n0_ref, m0_ref, gh_ref,
                      hn_ref, c_ref, n_ref, m_ref, *, lc):
    c_idx = pl.program_id(1)

    @pl.when(c_idx == 0)
    def _():
        c_ref[...] = c0_ref[...]
        n_ref[...] = n0_ref[...]
        m_ref[...] = m0_ref[...]

    row = lax.broadcasted_iota(jnp.int32, (lc, lc), 0)
    col = lax.broadcasted_iota(jnp.int32, (lc, lc), 1)
    causal = col <= row
    gcol = gcol_ref[0]
    grow = grow_ref[0]
    for h in range(A_HEADS):
        sl = slice(h * A_DH, (h + 1) * A_DH)
        qh = q_ref[0, :, sl]
        kh = k_ref[0, :, sl]
        vh = v_ref[0, :, sl]
        li_c = gcol[:, h:h + 1]
        lf_c = gcol[:, A_HEADS + h:A_HEADS + h + 1]
        li_r = grow[h:h + 1, :]
        lf_r = grow[A_HEADS + h:A_HEADS + h + 1, :]
        b_c = jnp.sum(jnp.where(causal, lf_r, 0.0), axis=1, keepdims=True)
        b_r = jnp.sum(jnp.where(row <= col, lf_c, 0.0), axis=0, keepdims=True)
        c_old = c_ref[0, h]
        n_old = n_ref[0, h:h + 1, :]
        m_old = m_ref[0, h:h + 1, 0:1]
        dmat = jnp.where(causal, b_c - b_r + li_r, NEG_BIG)
        inter = b_c + m_old
        m_t = jnp.maximum(inter, jnp.max(dmat, axis=1, keepdims=True))
        w_inter = jnp.exp(inter - m_t)
        s = lax.dot_general(qh, kh, (((1,), (1,)), ((), ())), preferred_element_type=F32)
        s = s * jnp.exp(dmat - m_t)
        num = w_inter * jnp.dot(qh, c_old.astype(BF16), preferred_element_type=F32)
        num = num + jnp.dot(s.astype(BF16), vh, preferred_element_type=F32)
        nq = w_inter * jnp.sum(qh.astype(F32) * n_old, axis=1, keepdims=True)
        nq = nq + jnp.sum(s, axis=1, keepdims=True)
        hv = num / jnp.maximum(jnp.abs(nq), jnp.exp(-m_t))
        mu = jnp.mean(hv, axis=1, keepdims=True)
        hc = hv - mu
        var = jnp.mean(hc * hc, axis=1, keepdims=True)
        hn_ref[0, :, sl] = hc * lax.rsqrt(var + EPS) * gh_ref[:, sl]
        m_new = m_t[lc - 1:lc, :]
        b_last = b_c[lc - 1:lc, :]
        g_state = jnp.exp(b_last + m_old - m_new)
        g_tok = jnp.exp(b_last - b_c + li_c - m_new)
        kw = kh.astype(F32) * g_tok
        c_ref[0, h] = g_state * c_old + jnp.dot(kw.T.astype(BF16), vh, preferred_element_type=F32)
        n_ref[0, h:h + 1, :] = g_state * n_old + jnp.sum(kw, axis=0, keepdims=True)
        m_ref[0, h:h + 1, :] = jnp.broadcast_to(m_new, (1, LANES))


def _mlstm_rec(q, k, v, gcol, grow, c0, n0, m0, g_head, lc_pref=256):
    bsz, t_len, _ = q.shape
    lc = _tile(t_len, lc_pref)
    act_spec = pl.BlockSpec((1, lc, A_INNER), lambda b, c: (b, c, 0))
    st_specs = [pl.BlockSpec((1, A_HEADS, A_DH, A_DH), lambda b, c: (b, 0, 0, 0)),
                pl.BlockSpec((1, A_HEADS, A_DH), lambda b, c: (b, 0, 0)),
                pl.BlockSpec((1, A_HEADS, LANES), lambda b, c: (b, 0, 0))]
    return pl.pallas_call(
        functools.partial(_mlstm_rec_kernel, lc=lc),
        grid=(bsz, t_len // lc),
        in_specs=[act_spec, act_spec, act_spec,
                  pl.BlockSpec((1, lc, LANES), lambda b, c: (b, c, 0)),
                  pl.BlockSpec((1, 8, lc), lambda b, c: (b, 0, c))] + st_specs +
                 [_const_spec((1, A_INNER))],
        out_specs=[act_spec] + st_specs,
        out_shape=[jax.ShapeDtypeStruct((bsz, t_len, A_INNER), F32),
                   jax.ShapeDtypeStruct((bsz, A_HEADS, A_DH, A_DH), F32),
                   jax.ShapeDtypeStruct((bsz, A_HEADS, A_DH), F32),
                   jax.ShapeDtypeStruct((bsz, A_HEADS, LANES), F32)],
        compiler_params=_cparams("parallel", "arbitrary"))(q, k, v, gcol, grow, c0, n0, m0, g_head)


def _mlstm_out_kernel(hn_ref, xc_ref, z_ref, skip_ref, w_ref, x_ref, o_ref):
    z = z_ref[...]
    a = (hn_ref[...] + skip_ref[...] * xc_ref[...].astype(F32)) * (1.0 / (1.0 + jnp.exp(-z)))
    o_ref[...] = x_ref[...] + jnp.dot(a.astype(BF16), w_ref[...], preferred_element_type=F32)


def _mlstm_out(hn, xc, up, skip, w_down, x, tm_pref=256):
    m, d = x.shape
    tm = _tile(m, tm_pref)
    row_spec = lambda n, blk: pl.BlockSpec((tm, n), lambda i: (i, blk))
    return pl.pallas_call(
        _mlstm_out_kernel, grid=(m // tm,),
        in_specs=[row_spec(A_INNER, 0), row_spec(A_INNER, 0), row_spec(A_INNER, 1),
                  _const_spec((1, A_INNER)), _const_spec(w_down.shape), row_spec(d, 0)],
        out_specs=row_spec(d, 0),
        out_shape=jax.ShapeDtypeStruct((m, d), F32),
        compiler_params=_cparams("parallel"))(hn, xc, up, skip, w_down, x)


def _sb_attn_kernel(q_ref, k_ref, v_ref, u_ref, o_ref, *, tq, tk, q_off):
    i = pl.program_id(2)
    q2 = q_ref[0]
    lane = lax.broadcasted_iota(jnp.int32, (tq, LANES), 1)
    qpos = q_off + i * tq + lax.broadcasted_iota(jnp.int32, (tq, tk), 0)
    kidx = lax.broadcasted_iota(jnp.int32, (tq, tk), 1)
    nkb = (q_off + i * tq + tq - 2) // tk + 1
    u = u_ref[...]
    zeros = jnp.zeros((tq, LANES), F32)
    outs = []
    for e in range(2):
        qm = jnp.where((lane >= HALF * e) & (lane < HALF * (e + 1)), q2, jnp.zeros_like(q2))

        def body(j, carry, qm=qm):
            acc, run = carry
            kb = nkb - 1 - j
            start = pl.multiple_of(kb * tk, tk)
            k2 = k_ref[0, pl.ds(start, tk), :]
            v2 = v_ref[0, pl.ds(start, tk), :]
            z = lax.dot_general(qm, k2, (((1,), (1,)), ((), ())), preferred_element_type=F32)
            before = (kidx + kb * tk) < qpos
            sp = jnp.maximum(z, 0.0) + jnp.log1p(jnp.exp(-jnp.abs(z)))
            lf = jnp.where(before, -sp, 0.0)
            hi = lf.astype(BF16)
            lo = (lf - hi.astype(F32)).astype(BF16)
            r = jnp.dot(jnp.concatenate([hi, lo], axis=1), u, preferred_element_type=F32)
            att = jnp.where(before, jnp.exp(z - sp + r[:, :tk] + run), 0.0)
            acc = acc + jnp.dot(att.astype(BF16), v2, preferred_element_type=F32)
            return acc, run + r[:, tk:]

        acc, _ = lax.fori_loop(0, nkb, body, (zeros, zeros))
        outs.append(acc)
    o_ref[0] = jnp.where(lane < HALF, outs[0], outs[1]).astype(o_ref.dtype)


def _sb_attn(q, k_all, v_all, u, q_off, tq_pref=256, tk=LANES):
    bsz, tq_len, d = q.shape
    tk_len = k_all.shape[1]
    tq = _tile(tq_len, tq_pref)
    assert tk_len % tk == 0 and (q_off + tq_len - 2) // tk < tk_len // tk
    return pl.pallas_call(
        functools.partial(_sb_attn_kernel, tq=tq, tk=tk, q_off=q_off),
        grid=(bsz, d // LANES, tq_len // tq),
        in_specs=[pl.BlockSpec((1, tq, LANES), lambda b, h, i: (b, i, h)),
                  pl.BlockSpec((1, tk_len, LANES), lambda b, h, i: (b, 0, h)),
                  pl.BlockSpec((1, tk_len, LANES), lambda b, h, i: (b, 0, h)),
                  _const_spec(u.shape)],
        out_specs=pl.BlockSpec((1, tq, LANES), lambda b, h, i: (b, i, h)),
        out_shape=jax.ShapeDtypeStruct((bsz, tq_len, d), BF16),
        compiler_params=_cparams("parallel", "parallel", "arbitrary"))(q, k_all, v_all, u)


def _rope128(x, cos_t, sin_s, lane):
    rot = jnp.where(lane < HALF + C_ROPE // 2, pltpu.roll(x, LANES - C_ROPE // 2, 1),
                    pltpu.roll(x, C_ROPE // 2, 1))
    return x * cos_t + rot * sin_s


def _mla_mid_kernel(cq_ref, ckv_ref, kr_ref, gq_ref, gkv_ref, wuq_ref, cos_ref, sin_ref,
                    q_ref, ckvn_ref, krp_ref):
    tm = cq_ref.shape[0]
    lane = lax.broadcasted_iota(jnp.int32, (tm, LANES), 1)
    cos_t = cos_ref[...]
    sin_s = sin_ref[...]
    cq = _rms(cq_ref[...], gq_ref[...]).astype(BF16)
    for h in range(C_HEADS):
        sl = slice(h * LANES, (h + 1) * LANES)
        qh = jnp.dot(cq, wuq_ref[:, sl], preferred_element_type=F32)
        q_ref[:, sl] = (_rope128(qh, cos_t, sin_s, lane) * C_SCALE).astype(BF16)
    ckvn_ref[...] = _rms(ckv_ref[...], gkv_ref[...])
    krp_ref[...] = _rope128(kr_ref[...], cos_t, sin_s, lane)


def _mla_mid(cq_raw, ckv_raw, krp_raw, g_q, g_kv, wuq, cos_t, sin_s, t_len, tm_pref=512):
    m = cq_raw.shape[0]
    tm = _tile(t_len, tm_pref)
    nt = t_len // tm
    rows = lambda n: pl.BlockSpec((tm, n), lambda i: (i, 0))
    tab = pl.BlockSpec((tm, LANES), lambda i: (i % nt, 0))
    return pl.pallas_call(
        _mla_mid_kernel, grid=(m // tm,),
        in_specs=[rows(cq_raw.shape[1]), rows(C_KV_RANK), rows(LANES),
                  _const_spec((1, cq_raw.shape[1])), _const_spec((1, C_KV_RANK)),
                  _const_spec(wuq.shape), tab, tab],
        out_specs=[rows(C_HEADS * LANES), rows(C_KV_RANK), rows(LANES)],
        out_shape=[jax.ShapeDtypeStruct((m, C_HEADS * LANES), BF16),
                   jax.ShapeDtypeStruct((m, C_KV_RANK), F32),
                   jax.ShapeDtypeStruct((m, LANES), F32)],
        compiler_params=_cparams("parallel"))(cq_raw, ckv_raw, krp_raw, g_q, g_kv, wuq, cos_t, sin_s)


def _mla_attn_kernel(q_ref, kv_ref, kr_ref, o_ref, *, tq, tk, q_off, n_valid):
    i = pl.program_id(2)
    lane_k = lax.broadcasted_iota(jnp.int32, (tk, LANES), 1)
    lane_q = lax.broadcasted_iota(jnp.int32, (tq, LANES), 1)
    qchunk = (q_off + i * tq + lax.broadcasted_iota(jnp.int32, (tq, tk), 0)) // CHUNK
    kidx = lax.broadcasted_iota(jnp.int32, (tq, tk), 1)
    n_vis = ((q_off + i * tq + tq - 1) // CHUNK + 1) * CHUNK
    nkb = (jnp.minimum(n_vis, n_valid) + tk - 1) // tk
    outs = []
    for e in range(2):
        sl = slice(e * LANES, (e + 1) * LANES)
        q = q_ref[0, :, sl]

        def body(kb, carry, q=q, sl=sl):
            m_run, l_run, acc = carry
            start = pl.multiple_of(kb * tk, tk)
            kvb = kv_ref[0, pl.ds(start, tk), sl]
            krb = kr_ref[0, pl.ds(start, tk), :]
            kf = jnp.where(lane_k < HALF, kvb, krb)
            s = lax.dot_general(q, kf, (((1,), (1,)), ((), ())), preferred_element_type=F32)
            kpos = kidx + kb * tk
            vis = ((kpos // CHUNK) <= qchunk) & (kpos < n_valid)
            s = jnp.where(vis, s, NEG_BIG)
            m_new = jnp.maximum(m_run, jnp.max(s, axis=1, keepdims=True))
            alpha = jnp.exp(m_run - m_new)
            p = jnp.exp(s - m_new)
            l_new = alpha * l_run + jnp.sum(p, axis=1, keepdims=True)
            acc = alpha * acc + jnp.dot(p.astype(BF16), kvb, preferred_element_type=F32)
            return m_new, l_new, acc

        init = (jnp.full((tq, 1), NEG_BIG, F32), jnp.zeros((tq, 1), F32), jnp.zeros((tq, LANES), F32))
        _, l_fin, acc = lax.fori_loop(0, nkb, body, init)
        outs.append(acc / l_fin)
    o = jnp.where(lane_q < HALF, pltpu.roll(outs[0], HALF, 1), outs[1])
    o_ref[0] = o.astype(o_ref.dtype)


def _mla_attn(q, kv, krp, q_off, n_valid, tq_pref=256, tk=LANES):
    bsz, tq_len, _ = q.shape
    tk_len = kv.shape[1]
    tq = _tile(tq_len, tq_pref)
    assert tk_len % tk == 0 and n_valid <= tk_len
    return pl.pallas_call(
        functools.partial(_mla_attn_kernel, tq=tq, tk=tk, q_off=q_off, n_valid=n_valid),
        grid=(bsz, C_HEADS // 2, tq_len // tq),
        in_specs=[pl.BlockSpec((1, tq, 2 * LANES), lambda b, h, i: (b, i, h)),
                  pl.BlockSpec((1, tk_len, 2 * LANES), lambda b, h, i: (b, 0, h)),
                  pl.BlockSpec((1, tk_len, LANES), lambda b, h, i: (b, 0, 0))],
        out_specs=pl.BlockSpec((1, tq, LANES), lambda b, h, i: (b, i, h)),
        out_shape=jax.ShapeDtypeStruct((bsz, tq_len, C_HEADS * C_VDIM), BF16),
        compiler_params=_cparams("parallel", "parallel", "arbitrary"))(q, kv, krp)


def _pad_rows(a, n):
    return jnp.pad(a, ((0, 0), (0, n - a.shape[1]), (0, 0)))


def _mlstm_layer(x, gain, st, w, bsz, t_len):
    (up,) = _proj(x, gain, [w["up"]], [F32])
    up3 = up.reshape(bsz, t_len, 2 * A_INNER)
    c0, n0, m0, conv0 = st
    conv0p = jnp.pad(conv0, ((0, 0), (8 - (A_CONV - 1), 0), (0, 0)))
    q, k, v, xc, gcol, grow = _mlstm_front(up3, conv0p, w["conv_w"], w["conv_b"], w["wq"], w["wk"],
                                           w["wv"], w["wg"], w["gb"])
    m0p = jnp.broadcast_to(m0[..., None], m0.shape + (LANES,))
    hn, c_new, n_new, m_new = _mlstm_rec(q, k, v, gcol, grow, c0, n0, m0p, w["g_head"])
    x = _mlstm_out(hn.reshape(bsz * t_len, A_INNER), xc.reshape(bsz * t_len, A_INNER), up,
                   w["skip"], w["down"], x)
    xm_tail = up3[:, t_len - (A_CONV - 1):, :A_INNER]
    if t_len >= A_CONV - 1:
        conv_new = xm_tail
    else:
        conv_new = jnp.concatenate([conv0, xm_tail], axis=1)[:, -(A_CONV - 1):]
    return x, (c_new, n_new, m_new[..., 0], conv_new)


def _sb_layer(x, gain, past, w, bsz, t_len, q_off):
    q, k, kb, v, vb = _proj(x, gain, [w["wq"], w["wk"], w["wv"]], [BF16, (F32, BF16), (F32, BF16)])
    d = q.shape[1]
    k3 = kb.reshape(bsz, t_len, d)
    v3 = vb.reshape(bsz, t_len, d)
    if past is not None:
        k3 = jnp.concatenate([past[0].reshape(bsz, -1, d).astype(BF16), k3], axis=1)
        v3 = jnp.concatenate([past[1].reshape(bsz, -1, d).astype(BF16), v3], axis=1)
    tk_pad = -(-k3.shape[1] // LANES) * LANES
    k3 = _pad_rows(k3, tk_pad)
    v3 = _pad_rows(v3, tk_pad)
    o = _sb_attn(q.reshape(bsz, t_len, d), k3, v3, w["u"], q_off)
    (x,) = _proj(o.reshape(bsz * t_len, d), None, [w["wo"]], [F32], residual=x)
    return x, (k.reshape(bsz, t_len, B_HEADS, B_DH), v.reshape(bsz, t_len, B_HEADS, B_DH))


def _rope_tables(t_len, q_off):
    half = C_ROPE // 2
    inv = ROPE_THETA ** (-jnp.arange(half, dtype=F32) / half)
    ang = (jnp.arange(t_len, dtype=F32) + q_off)[:, None] * inv[None, :]
    cos, sin = jnp.cos(ang), jnp.sin(ang)
    ones = jnp.ones((t_len, HALF), F32)
    zeros = jnp.zeros((t_len, HALF), F32)
    tail = LANES - HALF - C_ROPE
    cos_t = jnp.concatenate([ones, cos, cos, ones[:, :tail]], axis=1)
    sin_s = jnp.concatenate([zeros, -sin, sin, zeros[:, :tail]], axis=1)
    return cos_t, sin_s


def _mla_layer(x, gain, past, w, bsz, t_len, q_off):
    cq_raw, ckv_raw, krp_raw = _proj(x, gain, [w["dq"], w["dckv"], w["dkr"]], [F32, F32, F32])
    cos_t, sin_s = _rope_tables(t_len, q_off)
    q, ckv, krp = _mla_mid(cq_raw, ckv_raw, krp_raw, w["g_q"], w["g_kv"], w["uq"], cos_t, sin_s, t_len)
    ckv3 = ckv.reshape(bsz, t_len, C_KV_RANK)
    krp3 = krp.reshape(bsz, t_len, LANES)
    kr_new = krp3[:, :, HALF:HALF + C_ROPE]
    if past is not None:
        ckv_all = jnp.concatenate([past[0], ckv3], axis=1)
        kr_pad = jnp.pad(past[1], ((0, 0), (0, 0), (HALF, LANES - HALF - C_ROPE)))
        krp_all = jnp.concatenate([kr_pad, krp3], axis=1)
    else:
        ckv_all, krp_all = ckv3, krp3
    n_valid = ckv_all.shape[1]
    tk_pad = -(-n_valid // LANES) * LANES
    ckv_all = _pad_rows(ckv_all, tk_pad)
    krp_all = _pad_rows(krp_all, tk_pad).astype(BF16)
    (kv,) = _proj(ckv_all.reshape(bsz * tk_pad, C_KV_RANK), None, [w["ukv"]], [BF16])
    o = _mla_attn(q.reshape(bsz, t_len, C_HEADS * LANES), kv.reshape(bsz, tk_pad, C_HEADS * LANES),
                  krp_all, q_off, n_valid)
    (x,) = _proj(o.reshape(bsz * t_len, C_HEADS * C_VDIM), None, [w["wo"]], [F32], residual=x)
    return x, (ckv3, kr_new)


def _trunk(x3, q_off, st_a, past_b, past_c, wts):
    bsz, t_len, d = x3.shape
    x = x3.reshape(bsz * t_len, d)
    new_a, new_b, new_c = [], [], []
    for i in range(DEPTH):
        j = i // 3
        if i % 3 == 0:
            x, st = _mlstm_layer(x, wts["norm_mix"][i], tuple(s[j] for s in st_a), wts["a"][j], bsz, t_len)
            new_a.append(st)
        elif i % 3 == 1:
            past = None if past_b is None else (past_b[0][j], past_b[1][j])
            x, kv = _sb_layer(x, wts["norm_mix"][i], past, wts["b"][j], bsz, t_len, q_off)
            new_b.append(kv)
        else:
            past = None if past_c is None else (past_c[0][j], past_c[1][j])
            x, lat = _mla_layer(x, wts["norm_mix"][i], past, wts["c"][j], bsz, t_len, q_off)
            new_c.append(lat)
        final = wts["norm_final"] if i == DEPTH - 1 else None
        x = _mlp(x, wts["norm_mlp"][i], wts["ff1"][i], wts["ff2"][i], final_gain=final)
    stack = lambda items, idx: jnp.stack([it[idx] for it in items])
    return (x.reshape(bsz, t_len, d),
            stack(new_a, 0), stack(new_a, 1), stack(new_a, 2), stack(new_a, 3),
            stack(new_b, 0), stack(new_b, 1), stack(new_c, 0), stack(new_c, 1))


def _prep_weights(norm_mix, norm_mlp, norm_final, a_w_up, a_conv_w, a_conv_b, a_w_q, a_w_k, a_w_v,
                  a_w_gate, a_b_i, a_b_f, a_g_head, a_skip, a_w_down, b_w_qkv, b_w_o, c_w_dq, c_g_q,
                  c_w_uq, c_w_dkv, c_g_kv, c_w_ukv, c_w_o, w_ff1, w_ff2):
    bf = lambda a: a.astype(BF16)
    a_layers = []
    for j in range(a_w_up.shape[0]):
        wg = a_w_gate[j].reshape(3, A_HEADS, A_DH, 2 * A_HEADS)
        wg = jnp.pad(wg, ((0, 0), (0, 0), (0, 0), (0, LANES - 2 * A_HEADS)))
        gb = jnp.pad(jnp.concatenate([a_b_i[j], a_b_f[j]]), (0, LANES - 2 * A_HEADS)).reshape(1, LANES)
        a_layers.append(dict(up=bf(a_w_up[j]), conv_w=a_conv_w[j], conv_b=a_conv_b[j].reshape(1, A_INNER),
                             wq=bf(a_w_q[j]), wk=bf(a_w_k[j]), wv=bf(a_w_v[j]), wg=bf(wg), gb=gb,
                             g_head=a_g_head[j].reshape(1, A_INNER), skip=a_skip[j].reshape(1, A_INNER),
                             down=bf(a_w_down[j])))
    d = b_w_o.shape[1]
    jj = lax.broadcasted_iota(jnp.int32, (2 * LANES, 2 * LANES), 0) % LANES
    ss = lax.broadcasted_iota(jnp.int32, (2 * LANES, 2 * LANES), 1)
    u = ((ss >= LANES) | (jj > ss)).astype(BF16)
    b_layers = []
    for j in range(b_w_qkv.shape[0]):
        b_layers.append(dict(wq=bf(b_w_qkv[j][:, :d] * (B_DH ** -0.5)), wk=bf(b_w_qkv[j][:, d:2 * d]),
                             wv=bf(b_w_qkv[j][:, 2 * d:]), wo=bf(b_w_o[j]), u=u))
    c_layers = []
    for j in range(c_w_dq.shape[0]):
        uq = c_w_uq[j].reshape(-1, C_HEADS, C_NOPE + C_ROPE)
        uq = jnp.pad(uq, ((0, 0), (0, 0), (0, LANES - C_NOPE - C_ROPE))).reshape(-1, C_HEADS * LANES)
        dkr = jnp.pad(c_w_dkv[j][:, C_KV_RANK:], ((0, 0), (HALF, LANES - HALF - C_ROPE)))
        c_layers.append(dict(dq=bf(c_w_dq[j]), dckv=bf(c_w_dkv[j][:, :C_KV_RANK]), dkr=bf(dkr),
                             g_q=c_g_q[j].reshape(1, -1), g_kv=c_g_kv[j].reshape(1, -1), uq=bf(uq),
                             ukv=bf(c_w_ukv[j].reshape(C_KV_RANK, C_HEADS * LANES)), wo=bf(c_w_o[j])))
    return dict(norm_mix=norm_mix, norm_mlp=norm_mlp, norm_final=norm_final, a=a_layers, b=b_layers,
                c=c_layers, ff1=[bf(w_ff1[i]) for i in range(DEPTH)], ff2=[bf(w_ff2[i]) for i in range(DEPTH)])


def kernel(x_prompt, x_sample, state_mlstm_C, state_mlstm_n, state_mlstm_m, state_mlstm_conv, cache_sb_k, cache_sb_v, cache_mla_ckv, cache_mla_krope, norm_mix, norm_mlp, norm_final, a_w_up, a_conv_w, a_conv_b, a_w_q, a_w_k, a_w_v, a_w_gate, a_b_i, a_b_f, a_g_head, a_skip, a_w_down, b_w_qkv, b_w_o, c_w_dq, c_g_q, c_w_uq, c_w_dkv, c_g_kv, c_w_ukv, c_w_o, w_ff1, w_ff2):
    wts = _prep_weights(norm_mix, norm_mlp, norm_final, a_w_up, a_conv_w, a_conv_b, a_w_q, a_w_k, a_w_v,
                        a_w_gate, a_b_i, a_b_f, a_g_head, a_skip, a_w_down, b_w_qkv, b_w_o, c_w_dq, c_g_q,
                        c_w_uq, c_w_dkv, c_g_kv, c_w_ukv, c_w_o, w_ff1, w_ff2)
    n_a = state_mlstm_C.shape[0]
    bp = x_prompt.shape[0]
    zero_state = (jnp.zeros((n_a, bp, A_HEADS, A_DH, A_DH), F32), jnp.zeros((n_a, bp, A_HEADS, A_DH), F32),
                  jnp.zeros((n_a, bp, A_HEADS), F32), jnp.zeros((n_a, bp, A_CONV - 1, A_INNER), F32))
    outs_p = _trunk(x_prompt, 0, zero_state, None, None, wts)
    past_len = cache_sb_k.shape[2]
    outs_s = _trunk(x_sample, past_len,
                    (state_mlstm_C, state_mlstm_n, state_mlstm_m, state_mlstm_conv),
                    (cache_sb_k, cache_sb_v), (cache_mla_ckv, cache_mla_krope), wts)
    y_p, rest_p = outs_p[0], outs_p[1:]
    y_s, rest_s = outs_s[0], outs_s[1:]
    return (y_p, y_s) + tuple(rest_p) + tuple(rest_s)
```

```python
import functools

import jax
import jax.numpy as jnp
from jax import lax
from jax.experimental import pallas as pl
from jax.experimental.pallas import tpu as pltpu

F32 = jnp.float32
BF16 = jnp.bfloat16

EPS = 1e-6
DEPTH = 4
CHUNK = 64
A_HEADS = 4
A_DH = 512
A_INNER = A_HEADS * A_DH
A_CONV = 4
B_HEADS = 16
B_DH = 64
C_HEADS = 16
C_NOPE = 64
C_ROPE = 32
C_VDIM = 64
C_KV_RANK = 256
C_SCALE = (C_NOPE + C_ROPE) ** -0.5
ROPE_THETA = 10000.0

LANES = 128
HALF = LANES // 2
NEG_BIG = -1e30
VMEM_LIMIT_BYTES = 56 * 1024 * 1024


def _cparams(*sem):
    return pltpu.CompilerParams(dimension_semantics=sem, vmem_limit_bytes=VMEM_LIMIT_BYTES)


def _tile(n, pref):
    if n <= pref:
        return n
    t = pref
    while n % t:
        t //= 2
    return t


def _const_spec(shape):
    nd = len(shape)
    return pl.BlockSpec(shape, lambda *_: (0,) * nd)


def _rms(x, g):
    return x * lax.rsqrt(jnp.mean(x * x, axis=-1, keepdims=True) + EPS) * g


def _log_sigmoid(x):
    return jnp.minimum(x, 0.0) - jnp.log1p(jnp.exp(-jnp.abs(x)))


def _proj_kernel(*refs, n_outs, has_norm, has_res):
    refs = list(refs)
    x_ref = refs.pop(0)
    g_ref = refs.pop(0) if has_norm else None
    w_refs = [refs.pop(0) for _ in n_outs]
    r_ref = refs.pop(0) if has_res else None
    if has_norm:
        h = _rms(x_ref[...].astype(F32), g_ref[...]).astype(BF16)
    else:
        h = x_ref[...].astype(BF16)
    for idx, w_ref in enumerate(w_refs):
        o_refs = [refs.pop(0) for _ in range(n_outs[idx])]
        n = w_ref.shape[1]
        c = n if n <= 512 else 512
        for n0 in range(0, n, c):
            y = jnp.dot(h, w_ref[:, n0:n0 + c], preferred_element_type=F32)
            if has_res and idx == 0:
                y = y + r_ref[:, n0:n0 + c]
            for o_ref in o_refs:
                o_ref[:, n0:n0 + c] = y.astype(o_ref.dtype)


def _proj(x, gain, weights, out_dtypes, residual=None, tm_pref=512, name="proj"):
    m, k = x.shape
    tm = _tile(m, tm_pref)
    in_specs = [pl.BlockSpec((tm, k), lambda i: (i, 0))]
    args = [x]
    if gain is not None:
        in_specs.append(_const_spec((1, k)))
        args.append(gain.reshape(1, k).astype(F32))
    for w in weights:
        in_specs.append(_const_spec(w.shape))
        args.append(w)
    if residual is not None:
        in_specs.append(pl.BlockSpec((tm, residual.shape[1]), lambda i: (i, 0)))
        args.append(residual)
    out_dtypes = [dt if isinstance(dt, tuple) else (dt,) for dt in out_dtypes]
    out_shape, out_specs = [], []
    for w, dts in zip(weights, out_dtypes):
        for dt in dts:
            out_shape.append(jax.ShapeDtypeStruct((m, w.shape[1]), dt))
            out_specs.append(pl.BlockSpec((tm, w.shape[1]), lambda i: (i, 0)))
    kern = functools.partial(_proj_kernel, n_outs=tuple(len(d) for d in out_dtypes),
                             has_norm=gain is not None, has_res=residual is not None)
    return pl.pallas_call(kern, name=name, grid=(m // tm,), in_specs=in_specs, out_specs=out_specs,
                          out_shape=out_shape, compiler_params=_cparams("parallel"))(*args)


def _mlp_kernel(*refs, final):
    if final:
        x_ref, g_ref, w1_ref, w2_ref, gf_ref, o_ref, h_ref, acc_ref = refs
    else:
        x_ref, g_ref, w1_ref, w2_ref, o_ref, h_ref, acc_ref = refs
    f = pl.program_id(1)

    @pl.when(f == 0)
    def _():
        x = x_ref[...]
        h_ref[...] = _rms(x, g_ref[...]).astype(BF16)
        acc_ref[...] = x

    a = jnp.dot(h_ref[...], w1_ref[...], preferred_element_type=F32)
    a = jnp.maximum(a, 0.0)
    a = a * a
    acc_ref[...] += jnp.dot(a.astype(BF16), w2_ref[...], preferred_element_type=F32)

    @pl.when(f == pl.num_programs(1) - 1)
    def _():
        y = acc_ref[...]
        if final:
            y = _rms(y, gf_ref[...])
        o_ref[...] = y


def _mlp(x, gain, w1, w2, final_gain=None, tm_pref=512, tf_pref=1024):
    m, d = x.shape
    dff = w1.shape[1]
    tm = _tile(m, tm_pref)
    tf = _tile(dff, tf_pref)
    in_specs = [pl.BlockSpec((tm, d), lambda i, f: (i, 0)),
                _const_spec((1, d)),
                pl.BlockSpec((d, tf), lambda i, f: (0, f)),
                pl.BlockSpec((tf, d), lambda i, f: (f, 0))]
    args = [x, gain.reshape(1, d), w1, w2]
    if final_gain is not None:
        in_specs.append(_const_spec((1, d)))
        args.append(final_gain.reshape(1, d))
    return pl.pallas_call(
        functools.partial(_mlp_kernel, final=final_gain is not None), name="mlp",
        grid=(m // tm, dff // tf), in_specs=in_specs,
        out_specs=pl.BlockSpec((tm, d), lambda i, f: (i, 0)),
        out_shape=jax.ShapeDtypeStruct((m, d), F32),
        scratch_shapes=[pltpu.VMEM((tm, d), BF16), pltpu.VMEM((tm, d), F32)],
        compiler_params=_cparams("parallel", "arbitrary"))(*args)


def _mlstm_front_kernel(xm_ref, halo_ref, conv0_ref, cw_ref, cb_ref, wq_ref, wk_ref, wv_ref, wg_ref,
                        gb_ref, q_ref, k_ref, v_ref, xc_ref, gcol_ref, grow_ref, xp_ref, *, tt):
    t = pl.program_id(1)
    xm = xm_ref[0]
    prev = jnp.where(t == 0, conv0_ref[0], halo_ref[0])
    xp_ref[0:8, :] = prev
    xp_ref[8:8 + tt, :] = xm
    xc = cb_ref[...] + xm * cw_ref[A_CONV - 1:A_CONV, :]
    for j in range(A_CONV - 1):
        xc = xc + xp_ref[5 + j:5 + j + tt, :] * cw_ref[j:j + 1, :]
    xc = xc / (1.0 + jnp.exp(-xc))
    xc_ref[0] = xc.astype(BF16)
    g = jnp.zeros((tt, LANES), F32)
    for h in range(A_HEADS):
        sl = slice(h * A_DH, (h + 1) * A_DH)
        xch = xc[:, sl].astype(BF16)
        xmh = xm[:, sl].astype(BF16)
        qh = jnp.dot(xch, wq_ref[h], preferred_element_type=F32).astype(BF16)
        kh = (jnp.dot(xch, wk_ref[h], preferred_element_type=F32) * (A_DH ** -0.5)).astype(BF16)
        vh = jnp.dot(xmh, wv_ref[h], preferred_element_type=F32).astype(BF16)
        q_ref[0, :, sl] = qh
        k_ref[0, :, sl] = kh
        v_ref[0, :, sl] = vh
        g = g + jnp.dot(qh, wg_ref[0, h], preferred_element_type=F32)
        g = g + jnp.dot(kh, wg_ref[1, h], preferred_element_type=F32)
        g = g + jnp.dot(vh, wg_ref[2, h], preferred_element_type=F32)
    g = g + gb_ref[...]
    lane = lax.broadcasted_iota(jnp.int32, (tt, LANES), 1)
    g = jnp.where(lane < A_HEADS, g, _log_sigmoid(g))
    gcol_ref[0] = g
    sel = (lax.broadcasted_iota(jnp.int32, (8, LANES), 0) ==
           lax.broadcasted_iota(jnp.int32, (8, LANES), 1)).astype(BF16)
    grow = jnp.zeros((8, tt), F32)
    rem = g
    for _ in range(3):
        part = rem.astype(BF16)
        grow = grow + lax.dot_general(sel, part, (((1,), (1,)), ((), ())), preferred_element_type=F32)
        rem = rem - part.astype(F32)
    grow_ref[0] = grow


def _mlstm_front(up, conv0p, cw, cb, wq, wk, wv, wg, gb, tt_pref=256):
    bsz, t_len, _ = up.shape
    tt = _tile(t_len, tt_pref)
    nblk8 = tt // 8
    act = lambda dt: jax.ShapeDtypeStruct((bsz, t_len, A_INNER), dt)
    act_spec = pl.BlockSpec((1, tt, A_INNER), lambda b, t: (b, t, 0))
    return pl.pallas_call(
        functools.partial(_mlstm_front_kernel, tt=tt), name="mlstm_front",
        grid=(bsz, t_len // tt),
        in_specs=[act_spec,
                  pl.BlockSpec((1, 8, A_INNER), lambda b, t: (b, jnp.maximum(t * nblk8 - 1, 0), 0)),
                  pl.BlockSpec((1, 8, A_INNER), lambda b, t: (b, 0, 0)),
                  _const_spec(cw.shape), _const_spec(cb.shape), _const_spec(wq.shape),
                  _const_spec(wk.shape), _const_spec(wv.shape), _const_spec(wg.shape),
                  _const_spec(gb.shape)],
        out_specs=[act_spec, act_spec, act_spec, act_spec,
                   pl.BlockSpec((1, tt, LANES), lambda b, t: (b, t, 0)),
                   pl.BlockSpec((1, 8, tt), lambda b, t: (b, 0, t))],
        out_shape=[act(BF16), act(BF16), act(BF16), act(BF16),
                   jax.ShapeDtypeStruct((bsz, t_len, LANES), F32),
                   jax.ShapeDtypeStruct((bsz, 8, t_len), F32)],
        scratch_shapes=[pltpu.VMEM((tt + 8, A_INNER), F32)],
        compiler_params=_cparams("parallel", "arbitrary"))(up, up, conv0p, cw, cb, wq, wk, wv, wg, gb)


def _mlstm_rec_kernel(q_ref, k_ref, v_ref, gcol_ref, grow_ref, c0_ref, n0_ref, m0_ref, gh_ref,
                      hn_ref, c_ref, n_ref, m_ref, *, lc):
    c_idx = pl.program_id(1)

    @pl.when(c_idx == 0)
    def _():
        c_ref[...] = c0_ref[...]
        n_ref[...] = n0_ref[...]
        m_ref[...] = m0_ref[...]

    row = lax.broadcasted_iota(jnp.int32, (lc, lc), 0)
    col = lax.broadcasted_iota(jnp.int32, (lc, lc), 1)
    causal = col <= row
    gcol = gcol_ref[0]
    grow = grow_ref[0]
    for h in range(A_HEADS):
        sl = slice(h * A_DH, (h + 1) * A_DH)
        qh = q_ref[0, :, sl]
        kh = k_ref[0, :, sl]
        vh = v_ref[0, :, sl]
        li_c = gcol[:, h:h + 1]
        lf_c = gcol[:, A_HEADS + h:A_HEADS + h + 1]
        li_r = grow[h:h + 1, :]
        lf_r = grow[A_HEADS + h:A_HEADS + h + 1, :]
        b_c = jnp.sum(jnp.where(causal, lf_r, 0.0), axis=1, keepdims=True)
        b_r = jnp.sum(jnp.where(row <= col, lf_c, 0.0), axis=0, keepdims=True)
        c_old = c_ref[0, h]
        n_old = n_ref[0, h:h + 1, :]
        m_old = m_ref[0, h:h + 1, 0:1]
        dmat = jnp.where(causal, b_c - b_r + li_r, NEG_BIG)
        inter = b_c + m_old
        m_t = jnp.maximum(inter, jnp.max(dmat, axis=1, keepdims=True))
        w_inter = jnp.exp(inter - m_t)
        s = lax.dot_general(qh, kh, (((1,), (1,)), ((), ())), preferred_element_type=F32)
        s = s * jnp.exp(dmat - m_t)
        num = w_inter * jnp.dot(qh, c_old.astype(BF16), preferred_element_type=F32)
        num = num + jnp.dot(s.astype(BF16), vh, preferred_element_type=F32)
        nq = w_inter * jnp.sum(qh.astype(F32) * n_old, axis=1, keepdims=True)
        nq = nq + jnp.sum(s, axis=1, keepdims=True)
        hv = num / jnp.maximum(jnp.abs(nq), jnp.exp(-m_t))
        mu = jnp.mean(hv, axis=1, keepdims=True)
        hc = hv - mu
        var = jnp.mean(hc * hc, axis=1, keepdims=True)
        hn_ref[0, :, sl] = hc * lax.rsqrt(var + EPS) * gh_ref[:, sl]
        m_new = m_t[lc - 1:lc, :]
        b_last = b_c[lc - 1:lc, :]
        g_state = jnp.exp(b_last + m_old - m_new)
        g_tok = jnp.exp(b_last - b_c + li_c - m_new)
        kw = kh.astype(F32) * g_tok
        c_ref[0, h] = g_state * c_old + jnp.dot(kw.T.astype(BF16), vh, preferred_element_type=F32)
        n_ref[0, h:h + 1, :] = g_state * n_old + jnp.sum(kw, axis=0, keepdims=True)
        m_ref[0, h:h + 1, :] = jnp.broadcast_to(m_new, (1, LANES))


def _mlstm_rec(q, k, v, gcol, grow, c0, n0, m0, g_head, lc_pref=256):
    bsz, t_len, _ = q.shape
    lc = _tile(t_len, lc_pref)
    act_spec = pl.BlockSpec((1, lc, A_INNER), lambda b, c: (b, c, 0))
    st_specs = [pl.BlockSpec((1, A_HEADS, A_DH, A_DH), lambda b, c: (b, 0, 0, 0)),
                pl.BlockSpec((1, A_HEADS, A_DH), lambda b, c: (b, 0, 0)),
                pl.BlockSpec((1, A_HEADS, LANES), lambda b, c: (b, 0, 0))]
    return pl.pallas_call(
        functools.partial(_mlstm_rec_kernel, lc=lc), name="mlstm_rec",
        grid=(bsz, t_len // lc),
        in_specs=[act_spec, act_spec, act_spec,
                  pl.BlockSpec((1, lc, LANES), lambda b, c: (b, c, 0)),
                  pl.BlockSpec((1, 8, lc), lambda b, c: (b, 0, c))] + st_specs +
                 [_const_spec((1, A_INNER))],
        out_specs=[act_spec] + st_specs,
        out_shape=[jax.ShapeDtypeStruct((bsz, t_len, A_INNER), F32),
                   jax.ShapeDtypeStruct((bsz, A_HEADS, A_DH, A_DH), F32),
                   jax.ShapeDtypeStruct((bsz, A_HEADS, A_DH), F32),
                   jax.ShapeDtypeStruct((bsz, A_HEADS, LANES), F32)],
        compiler_params=_cparams("parallel", "arbitrary"))(q, k, v, gcol, grow, c0, n0, m0, g_head)


def _mlstm_out_kernel(hn_ref, xc_ref, z_ref, skip_ref, w_ref, x_ref, o_ref):
    z = z_ref[...]
    a = (hn_ref[...] + skip_ref[...] * xc_ref[...].astype(F32)) * (1.0 / (1.0 + jnp.exp(-z)))
    o_ref[...] = x_ref[...] + jnp.dot(a.astype(BF16), w_ref[...], preferred_element_type=F32)


def _mlstm_out(hn, xc, up, skip, w_down, x, tm_pref=256):
    m, d = x.shape
    tm = _tile(m, tm_pref)
    row_spec = lambda n, blk: pl.BlockSpec((tm, n), lambda i: (i, blk))
    return pl.pallas_call(
        _mlstm_out_kernel, name="mlstm_out", grid=(m // tm,),
        in_specs=[row_spec(A_INNER, 0), row_spec(A_INNER, 0), row_spec(A_INNER, 1),
                  _const_spec((1, A_INNER)), _const_spec(w_down.shape), row_spec(d, 0)],
        out_specs=row_spec(d, 0),
        out_shape=jax.ShapeDtypeStruct((m, d), F32),
        compiler_params=_cparams("parallel"))(hn, xc, up, skip, w_down, x)


def _sb_attn_kernel(q_ref, k_ref, v_ref, u_ref, o_ref, *, tq, tk, q_off):
    i = pl.program_id(2)
    nsub = tk // LANES
    q2 = q_ref[0]
    lane = lax.broadcasted_iota(jnp.int32, (tq, LANES), 1)
    qpos = q_off + i * tq + lax.broadcasted_iota(jnp.int32, (tq, tk), 0)
    kidx = lax.broadcasted_iota(jnp.int32, (tq, tk), 1)
    nkb = (q_off + i * tq + tq - 2) // tk + 1
    u = u_ref[...]
    zeros = jnp.zeros((tq, LANES), F32)
    qms = [jnp.where((lane >= HALF * e) & (lane < HALF * (e + 1)), q2, jnp.zeros_like(q2)) for e in range(2)]

    def body(j, carry):
        kb = nkb - 1 - j
        start = pl.multiple_of(kb * tk, tk)
        k2 = k_ref[0, pl.ds(start, tk), :]
        v2 = v_ref[0, pl.ds(start, tk), :]
        before = (kidx + start) < qpos
        new = []
        for e in range(2):
            acc, run = carry[2 * e], carry[2 * e + 1]
            z = lax.dot_general(qms[e], k2, (((1,), (1,)), ((), ())), preferred_element_type=F32)
            sp = jnp.maximum(z, 0.0) + jnp.log(1.0 + jnp.exp(-jnp.abs(z)))
            nlf = jnp.where(before, sp, 0.0)
            rests = [None] * nsub
            for sb in reversed(range(nsub)):
                blk = nlf[:, sb * LANES:(sb + 1) * LANES]
                hi = blk.astype(BF16)
                lo = (blk - hi.astype(F32)).astype(BF16)
                r = jnp.dot(jnp.concatenate([hi, lo], axis=1), u, preferred_element_type=F32)
                rests[sb] = r[:, :LANES] + run
                run = run + r[:, LANES:]
            rest = rests[0] if nsub == 1 else jnp.concatenate(rests, axis=1)
            att = jnp.where(before, jnp.exp(z - sp + rest), 0.0)
            acc = acc + jnp.dot(att.astype(BF16), v2, preferred_element_type=F32)
            new += [acc, run]
        return tuple(new)

    res = lax.fori_loop(0, nkb, body, (zeros,) * 4)
    o_ref[0] = jnp.where(lane < HALF, res[0], res[2]).astype(o_ref.dtype)


def _key_tile(tk_len):
    return tk_len if tk_len <= 10 * LANES else 2 * LANES


def _sb_attn(q, k_all, v_all, u, q_off, tq_pref=512):
    bsz, tq_len, d = q.shape
    tk_len = k_all.shape[1]
    tq = _tile(tq_len, tq_pref)
    tk = _key_tile(tk_len)
    assert tk_len % tk == 0 and (q_off + tq_len - 2) // tk < tk_len // tk
    return pl.pallas_call(
        functools.partial(_sb_attn_kernel, tq=tq, tk=tk, q_off=q_off), name="sb_attn",
        grid=(bsz, d // LANES, tq_len // tq),
        in_specs=[pl.BlockSpec((1, tq, LANES), lambda b, h, i: (b, i, h)),
                  pl.BlockSpec((1, tk_len, LANES), lambda b, h, i: (b, 0, h)),
                  pl.BlockSpec((1, tk_len, LANES), lambda b, h, i: (b, 0, h)),
                  _const_spec(u.shape)],
        out_specs=pl.BlockSpec((1, tq, LANES), lambda b, h, i: (b, i, h)),
        out_shape=jax.ShapeDtypeStruct((bsz, tq_len, d), BF16),
        compiler_params=_cparams("parallel", "parallel", "arbitrary"))(q, k_all, v_all, u)


def _rope128(x, cos_t, sin_s, lane):
    rot = jnp.where(lane < HALF + C_ROPE // 2, pltpu.roll(x, LANES - C_ROPE // 2, 1),
                    pltpu.roll(x, C_ROPE // 2, 1))
    return x * cos_t + rot * sin_s


def _mla_mid_kernel(cq_ref, ckv_ref, kr_ref, gq_ref, gkv_ref, wuq_ref, cos_ref, sin_ref,
                    q_ref, ckvn_ref, krp_ref):
    tm = cq_ref.shape[0]
    lane = lax.broadcasted_iota(jnp.int32, (tm, LANES), 1)
    cos_t = cos_ref[...]
    sin_s = sin_ref[...]
    cq = _rms(cq_ref[...], gq_ref[...]).astype(BF16)
    for h in range(C_HEADS):
        sl = slice(h * LANES, (h + 1) * LANES)
        qh = jnp.dot(cq, wuq_ref[:, sl], preferred_element_type=F32)
        q_ref[:, sl] = (_rope128(qh, cos_t, sin_s, lane) * C_SCALE).astype(BF16)
    ckvn_ref[...] = _rms(ckv_ref[...], gkv_ref[...])
    krp_ref[...] = _rope128(kr_ref[...], cos_t, sin_s, lane)


def _mla_mid(cq_raw, ckv_raw, krp_raw, g_q, g_kv, wuq, cos_t, sin_s, t_len, tm_pref=512):
    m = cq_raw.shape[0]
    tm = _tile(t_len, tm_pref)
    nt = t_len // tm
    rows = lambda n: pl.BlockSpec((tm, n), lambda i: (i, 0))
    tab = pl.BlockSpec((tm, LANES), lambda i: (i % nt, 0))
    return pl.pallas_call(
        _mla_mid_kernel, name="mla_mid", grid=(m // tm,),
        in_specs=[rows(cq_raw.shape[1]), rows(C_KV_RANK), rows(LANES),
                  _const_spec((1, cq_raw.shape[1])), _const_spec((1, C_KV_RANK)),
                  _const_spec(wuq.shape), tab, tab],
        out_specs=[rows(C_HEADS * LANES), rows(C_KV_RANK), rows(LANES)],
        out_shape=[jax.ShapeDtypeStruct((m, C_HEADS * LANES), BF16),
                   jax.ShapeDtypeStruct((m, C_KV_RANK), F32),
                   jax.ShapeDtypeStruct((m, LANES), F32)],
        compiler_params=_cparams("parallel"))(cq_raw, ckv_raw, krp_raw, g_q, g_kv, wuq, cos_t, sin_s)


def _mla_attn_kernel(q_ref, kv_ref, kr_ref, o_ref, *, tq, tk, q_off, n_valid):
    i = pl.program_id(2)
    lane_k = lax.broadcasted_iota(jnp.int32, (tk, LANES), 1)
    lane_q = lax.broadcasted_iota(jnp.int32, (tq, LANES), 1)
    qpos = q_off + i * tq + lax.broadcasted_iota(jnp.int32, (tq, tk), 0)
    klim = jnp.minimum((qpos | (CHUNK - 1)) + 1, n_valid)
    kidx = lax.broadcasted_iota(jnp.int32, (tq, tk), 1)
    n_vis = ((q_off + i * tq + tq - 1) // CHUNK + 1) * CHUNK
    nkb = (jnp.minimum(n_vis, n_valid) + tk - 1) // tk
    qs = [q_ref[0, :, e * LANES:(e + 1) * LANES] for e in range(2)]

    def body(kb, carry):
        start = pl.multiple_of(kb * tk, tk)
        krb = kr_ref[0, pl.ds(start, tk), :]
        vis = (kidx + start) < klim
        new = []
        for e in range(2):
            m_run, l_run, acc = carry[3 * e:3 * e + 3]
            kvb = kv_ref[0, pl.ds(start, tk), e * LANES:(e + 1) * LANES]
            kf = jnp.where(lane_k < HALF, kvb, krb)
            s = lax.dot_general(qs[e], kf, (((1,), (1,)), ((), ())), preferred_element_type=F32)
            s = jnp.where(vis, s, NEG_BIG)
            m_new = jnp.maximum(m_run, jnp.max(s, axis=1, keepdims=True))
            alpha = jnp.exp(m_run - m_new)
            p = jnp.exp(s - m_new)
            l_new = alpha * l_run + jnp.sum(p, axis=1, keepdims=True)
            acc = alpha * acc + jnp.dot(p.astype(BF16), kvb, preferred_element_type=F32)
            new += [m_new, l_new, acc]
        return tuple(new)

    init = (jnp.full((tq, 1), NEG_BIG, F32), jnp.zeros((tq, 1), F32), jnp.zeros((tq, LANES), F32)) * 2
    res = lax.fori_loop(0, nkb, body, init)
    o0 = res[2] / res[1]
    o1 = res[5] / res[4]
    o_ref[0] = jnp.where(lane_q < HALF, pltpu.roll(o0, HALF, 1), o1).astype(o_ref.dtype)


def _mla_attn(q, kv, krp, q_off, n_valid, tq_pref=512):
    bsz, tq_len, _ = q.shape
    tk_len = kv.shape[1]
    tq = _tile(tq_len, tq_pref)
    tk = _key_tile(tk_len)
    assert CHUNK & (CHUNK - 1) == 0 and tk_len % tk == 0 and n_valid <= tk_len
    return pl.pallas_call(
        functools.partial(_mla_attn_kernel, tq=tq, tk=tk, q_off=q_off, n_valid=n_valid), name="mla_attn",
        grid=(bsz, C_HEADS // 2, tq_len // tq),
        in_specs=[pl.BlockSpec((1, tq, 2 * LANES), lambda b, h, i: (b, i, h)),
                  pl.BlockSpec((1, tk_len, 2 * LANES), lambda b, h, i: (b, 0, h)),
                  pl.BlockSpec((1, tk_len, LANES), lambda b, h, i: (b, 0, 0))],
        out_specs=pl.BlockSpec((1, tq, LANES), lambda b, h, i: (b, i, h)),
        out_shape=jax.ShapeDtypeStruct((bsz, tq_len, C_HEADS * C_VDIM), BF16),
        compiler_params=_cparams("parallel", "parallel", "arbitrary"))(q, kv, krp)


def _pad_rows(a, n):
    return jnp.pad(a, ((0, 0), (0, n - a.shape[1]), (0, 0)))


def _mlstm_layer(x, gain, st, w, bsz, t_len):
    (up,) = _proj(x, gain, [w["up"]], [F32], name="mlstm_up")
    up3 = up.reshape(bsz, t_len, 2 * A_INNER)
    c0, n0, m0, conv0 = st
    conv0p = jnp.pad(conv0, ((0, 0), (8 - (A_CONV - 1), 0), (0, 0)))
    q, k, v, xc, gcol, grow = _mlstm_front(up3, conv0p, w["conv_w"], w["conv_b"], w["wq"], w["wk"],
                                           w["wv"], w["wg"], w["gb"])
    m0p = jnp.broadcast_to(m0[..., None], m0.shape + (LANES,))
    hn, c_new, n_new, m_new = _mlstm_rec(q, k, v, gcol, grow, c0, n0, m0p, w["g_head"])
    x = _mlstm_out(hn.reshape(bsz * t_len, A_INNER), xc.reshape(bsz * t_len, A_INNER), up,
                   w["skip"], w["down"], x)
    assert t_len >= A_CONV - 1
    conv_new = up3[:, t_len - (A_CONV - 1):, :A_INNER]
    return x, (c_new, n_new, m_new[..., 0], conv_new)


def _sb_layer(x, gain, past, w, bsz, t_len, q_off):
    q, k, kb, v, vb = _proj(x, gain, [w["wq"], w["wk"], w["wv"]], [BF16, (F32, BF16), (F32, BF16)],
                            name="sb_qkv")
    d = q.shape[1]
    k3 = kb.reshape(bsz, t_len, d)
    v3 = vb.reshape(bsz, t_len, d)
    if past is not None:
        k3 = jnp.concatenate([past[0].reshape(bsz, -1, d).astype(BF16), k3], axis=1)
        v3 = jnp.concatenate([past[1].reshape(bsz, -1, d).astype(BF16), v3], axis=1)
    tk_pad = -(-k3.shape[1] // LANES) * LANES
    k3 = _pad_rows(k3, tk_pad)
    v3 = _pad_rows(v3, tk_pad)
    o = _sb_attn(q.reshape(bsz, t_len, d), k3, v3, w["u"], q_off)
    (x,) = _proj(o.reshape(bsz * t_len, d), None, [w["wo"]], [F32], residual=x, name="sb_out")
    return x, (k.reshape(bsz, t_len, B_HEADS, B_DH), v.reshape(bsz, t_len, B_HEADS, B_DH))


def _rope_tables(t_len, q_off):
    half = C_ROPE // 2
    inv = ROPE_THETA ** (-jnp.arange(half, dtype=F32) / half)
    ang = (jnp.arange(t_len, dtype=F32) + q_off)[:, None] * inv[None, :]
    cos, sin = jnp.cos(ang), jnp.sin(ang)
    ones = jnp.ones((t_len, HALF), F32)
    zeros = jnp.zeros((t_len, HALF), F32)
    tail = LANES - HALF - C_ROPE
    cos_t = jnp.concatenate([ones, cos, cos, ones[:, :tail]], axis=1)
    sin_s = jnp.concatenate([zeros, -sin, sin, zeros[:, :tail]], axis=1)
    return cos_t, sin_s


def _mla_layer(x, gain, past, w, bsz, t_len, q_off):
    cq_raw, ckv_raw, krp_raw = _proj(x, gain, [w["dq"], w["dckv"], w["dkr"]], [F32, F32, F32],
                                     name="mla_down")
    cos_t, sin_s = _rope_tables(t_len, q_off)
    q, ckv, krp = _mla_mid(cq_raw, ckv_raw, krp_raw, w["g_q"], w["g_kv"], w["uq"], cos_t, sin_s, t_len)
    ckv3 = ckv.reshape(bsz, t_len, C_KV_RANK)
    krp3 = krp.reshape(bsz, t_len, LANES)
    kr_new = krp3[:, :, HALF:HALF + C_ROPE]
    if past is not None:
        ckv_all = jnp.concatenate([past[0], ckv3], axis=1)
        kr_pad = jnp.pad(past[1], ((0, 0), (0, 0), (HALF, LANES - HALF - C_ROPE)))
        krp_all = jnp.concatenate([kr_pad, krp3], axis=1)
    else:
        ckv_all, krp_all = ckv3, krp3
    n_valid = ckv_all.shape[1]
    tk_pad = -(-n_valid // LANES) * LANES
    ckv_all = _pad_rows(ckv_all, tk_pad)
    krp_all = _pad_rows(krp_all, tk_pad).astype(BF16)
    (kv,) = _proj(ckv_all.reshape(bsz * tk_pad, C_KV_RANK), None, [w["ukv"]], [BF16], name="mla_kv_up")
    o = _mla_attn(q.reshape(bsz, t_len, C_HEADS * LANES), kv.reshape(bsz, tk_pad, C_HEADS * LANES),
                  krp_all, q_off, n_valid)
    (x,) = _proj(o.reshape(bsz * t_len, C_HEADS * C_VDIM), None, [w["wo"]], [F32], residual=x,
                 name="mla_out")
    return x, (ckv3, kr_new)


def _trunk(x3, q_off, st_a, past_b, past_c, wts):
    bsz, t_len, d = x3.shape
    x = x3.reshape(bsz * t_len, d)
    new_a, new_b, new_c = [], [], []
    for i in range(DEPTH):
        j = i // 3
        if i % 3 == 0:
            x, st = _mlstm_layer(x, wts["norm_mix"][i], tuple(s[j] for s in st_a), wts["a"][j], bsz, t_len)
            new_a.append(st)
        elif i % 3 == 1:
            past = None if past_b is None else (past_b[0][j], past_b[1][j])
            x, kv = _sb_layer(x, wts["norm_mix"][i], past, wts["b"][j], bsz, t_len, q_off)
            new_b.append(kv)
        else:
            past = None if past_c is None else (past_c[0][j], past_c[1][j])
            x, lat = _mla_layer(x, wts["norm_mix"][i], past, wts["c"][j], bsz, t_len, q_off)
            new_c.append(lat)
        final = wts["norm_final"] if i == DEPTH - 1 else None
        x = _mlp(x, wts["norm_mlp"][i], wts["ff1"][i], wts["ff2"][i], final_gain=final)
    stack = lambda items, idx: jnp.stack([it[idx] for it in items])
    return (x.reshape(bsz, t_len, d),
            stack(new_a, 0), stack(new_a, 1), stack(new_a, 2), stack(new_a, 3),
            stack(new_b, 0), stack(new_b, 1), stack(new_c, 0), stack(new_c, 1))


def _prep_weights(norm_mix, norm_mlp, norm_final, a_w_up, a_conv_w, a_conv_b, a_w_q, a_w_k, a_w_v,
                  a_w_gate, a_b_i, a_b_f, a_g_head, a_skip, a_w_down, b_w_qkv, b_w_o, c_w_dq, c_g_q,
                  c_w_uq, c_w_dkv, c_g_kv, c_w_ukv, c_w_o, w_ff1, w_ff2):
    bf = lambda a: a.astype(BF16)
    a_layers = []
    for j in range(a_w_up.shape[0]):
        wg = a_w_gate[j].reshape(3, A_HEADS, A_DH, 2 * A_HEADS)
        wg = jnp.pad(wg, ((0, 0), (0, 0), (0, 0), (0, LANES - 2 * A_HEADS)))
        gb = jnp.pad(jnp.concatenate([a_b_i[j], a_b_f[j]]), (0, LANES - 2 * A_HEADS)).reshape(1, LANES)
        a_layers.append(dict(up=bf(a_w_up[j]), conv_w=a_conv_w[j], conv_b=a_conv_b[j].reshape(1, A_INNER),
                             wq=bf(a_w_q[j]), wk=bf(a_w_k[j]), wv=bf(a_w_v[j]), wg=bf(wg), gb=gb,
                             g_head=a_g_head[j].reshape(1, A_INNER), skip=a_skip[j].reshape(1, A_INNER),
                             down=bf(a_w_down[j])))
    d = b_w_o.shape[1]
    jj = lax.broadcasted_iota(jnp.int32, (2 * LANES, 2 * LANES), 0) % LANES
    ss = lax.broadcasted_iota(jnp.int32, (2 * LANES, 2 * LANES), 1)
    u = -((ss >= LANES) | (jj > ss)).astype(BF16)
    b_layers = []
    for j in range(b_w_qkv.shape[0]):
        b_layers.append(dict(wq=bf(b_w_qkv[j][:, :d] * (B_DH ** -0.5)), wk=bf(b_w_qkv[j][:, d:2 * d]),
                             wv=bf(b_w_qkv[j][:, 2 * d:]), wo=bf(b_w_o[j]), u=u))
    c_layers = []
    for j in range(c_w_dq.shape[0]):
        uq = c_w_uq[j].reshape(-1, C_HEADS, C_NOPE + C_ROPE)
        uq = jnp.pad(uq, ((0, 0), (0, 0), (0, LANES - C_NOPE - C_ROPE))).reshape(-1, C_HEADS * LANES)
        dkr = jnp.pad(c_w_dkv[j][:, C_KV_RANK:], ((0, 0), (HALF, LANES - HALF - C_ROPE)))
        c_layers.append(dict(dq=bf(c_w_dq[j]), dckv=bf(c_w_dkv[j][:, :C_KV_RANK]), dkr=bf(dkr),
                             g_q=c_g_q[j].reshape(1, -1), g_kv=c_g_kv[j].reshape(1, -1), uq=bf(uq),
                             ukv=bf(c_w_ukv[j].reshape(C_KV_RANK, C_HEADS * LANES)), wo=bf(c_w_o[j])))
    return dict(norm_mix=norm_mix, norm_mlp=norm_mlp, norm_final=norm_final, a=a_layers, b=b_layers,
                c=c_layers, ff1=[bf(w_ff1[i]) for i in range(DEPTH)], ff2=[bf(w_ff2[i]) for i in range(DEPTH)])


def kernel(x_prompt, x_sample, state_mlstm_C, state_mlstm_n, state_mlstm_m, state_mlstm_conv, cache_sb_k, cache_sb_v, cache_mla_ckv, cache_mla_krope, norm_mix, norm_mlp, norm_final, a_w_up, a_conv_w, a_conv_b, a_w_q, a_w_k, a_w_v, a_w_gate, a_b_i, a_b_f, a_g_head, a_skip, a_w_down, b_w_qkv, b_w_o, c_w_dq, c_g_q, c_w_uq, c_w_dkv, c_g_kv, c_w_ukv, c_w_o, w_ff1, w_ff2):
    wts = _prep_weights(norm_mix, norm_mlp, norm_final, a_w_up, a_conv_w, a_conv_b, a_w_q, a_w_k, a_w_v,
                        a_w_gate, a_b_i, a_b_f, a_g_head, a_skip, a_w_down, b_w_qkv, b_w_o, c_w_dq, c_g_q,
                        c_w_uq, c_w_dkv, c_g_kv, c_w_ukv, c_w_o, w_ff1, w_ff2)
    n_a = state_mlstm_C.shape[0]
    bp = x_prompt.shape[0]
    zero_state = (jnp.zeros((n_a, bp, A_HEADS, A_DH, A_DH), F32), jnp.zeros((n_a, bp, A_HEADS, A_DH), F32),
                  jnp.zeros((n_a, bp, A_HEADS), F32), jnp.zeros((n_a, bp, A_CONV - 1, A_INNER), F32))
    outs_p = _trunk(x_prompt, 0, zero_state, None, None, wts)
    past_len = cache_sb_k.shape[2]
    outs_s = _trunk(x_sample, past_len,
                    (state_mlstm_C, state_mlstm_n, state_mlstm_m, state_mlstm_conv),
                    (cache_sb_k, cache_sb_v), (cache_mla_ckv, cache_mla_krope), wts)
    y_p, rest_p = outs_p[0], outs_p[1:]
    y_s, rest_s = outs_s[0], outs_s[1:]
    return (y_p, y_s) + tuple(rest_p) + tuple(rest_s)
```

```python
import functools

import jax
import jax.numpy as jnp
from jax import lax
from jax.experimental import pallas as pl
from jax.experimental.pallas import tpu as pltpu

F32 = jnp.float32
BF16 = jnp.bfloat16

EPS = 1e-6
DEPTH = 4
CHUNK = 64
A_HEADS = 4
A_DH = 512
A_INNER = A_HEADS * A_DH
A_CONV = 4
B_HEADS = 16
B_DH = 64
C_HEADS = 16
C_NOPE = 64
C_ROPE = 32
C_VDIM = 64
C_KV_RANK = 256
C_SCALE = (C_NOPE + C_ROPE) ** -0.5
ROPE_THETA = 10000.0

LANES = 128
HALF = LANES // 2
NEG_BIG = -1e30
LOG2E = 1.4426950408889634
VMEM_LIMIT_BYTES = 56 * 1024 * 1024


def _cparams(*sem):
    return pltpu.CompilerParams(dimension_semantics=sem, vmem_limit_bytes=VMEM_LIMIT_BYTES)


def _tile(n, pref):
    if n <= pref:
        return n
    t = pref
    while n % t:
        t //= 2
    return t


def _const_spec(shape, layer=None):
    nd = len(shape)
    if layer is None:
        return pl.BlockSpec(shape, lambda *_: (0,) * nd)
    return pl.BlockSpec((None,) + tuple(shape[1:]), lambda *_: (layer,) + (0,) * (nd - 1))


def _rms(x, g):
    return x * lax.rsqrt(jnp.mean(x * x, axis=-1, keepdims=True) + EPS) * g


def _log_sigmoid(x):
    return jnp.minimum(x, 0.0) - jnp.log1p(jnp.exp(-jnp.abs(x)))


def _proj_kernel(*refs, n_w, outs, has_norm, has_res):
    refs = list(refs)
    x_ref = refs.pop(0)
    g_ref = refs.pop(0) if has_norm else None
    w_refs = [refs.pop(0) for _ in range(n_w)]
    r_ref = refs.pop(0) if has_res else None
    if has_norm:
        h = _rms(x_ref[...].astype(F32), g_ref[...]).astype(BF16)
    else:
        h = x_ref[...].astype(BF16)
    groups = {}
    for o_ref, (wi, c0, c1, _) in zip(refs, outs):
        groups.setdefault((wi, c0, c1), []).append(o_ref)
    for gi, ((wi, c0, c1), dests) in enumerate(groups.items()):
        n = c1 - c0
        c = n if n <= 512 else 512
        for n0 in range(0, n, c):
            y = jnp.dot(h, w_refs[wi][:, c0 + n0:c0 + n0 + c], preferred_element_type=F32)
            if has_res and gi == 0:
                y = y + r_ref[:, n0:n0 + c]
            for o_ref in dests:
                o_ref[:, n0:n0 + c] = y.astype(o_ref.dtype)


def _proj(x, gain, weights, outs, residual=None, tm_pref=512, name="proj"):
    m, k = x.shape
    tm = _tile(m, tm_pref)
    in_specs = [pl.BlockSpec((tm, k), lambda i: (i, 0))]
    args = [x]
    if gain is not None:
        in_specs.append(_const_spec((1, k)))
        args.append(gain.reshape(1, k).astype(F32))
    for w, layer in weights:
        in_specs.append(_const_spec(w.shape, layer))
        args.append(w)
    if residual is not None:
        in_specs.append(pl.BlockSpec((tm, residual.shape[1]), lambda i: (i, 0)))
        args.append(residual)
    out_shape = [jax.ShapeDtypeStruct((m, c1 - c0), dt) for _, c0, c1, dt in outs]
    out_specs = [pl.BlockSpec((tm, c1 - c0), lambda i: (i, 0)) for _, c0, c1, _ in outs]
    kern = functools.partial(_proj_kernel, n_w=len(weights), outs=tuple(outs),
                             has_norm=gain is not None, has_res=residual is not None)
    return pl.pallas_call(kern, name=name, grid=(m // tm,), in_specs=in_specs, out_specs=out_specs,
                          out_shape=out_shape, compiler_params=_cparams("parallel"))(*args)


def _mlp_kernel(*refs, final):
    if final:
        x_ref, g_ref, w1_ref, w2_ref, gf_ref, o_ref, h_ref, acc_ref = refs
    else:
        x_ref, g_ref, w1_ref, w2_ref, o_ref, h_ref, acc_ref = refs
    f = pl.program_id(1)

    @pl.when(f == 0)
    def _():
        x = x_ref[...]
        h_ref[...] = _rms(x, g_ref[...]).astype(BF16)
        acc_ref[...] = x

    a = jnp.dot(h_ref[...], w1_ref[...], preferred_element_type=F32)
    a = jnp.maximum(a, 0.0)
    a = a * a
    acc_ref[...] += jnp.dot(a.astype(BF16), w2_ref[...], preferred_element_type=F32)

    @pl.when(f == pl.num_programs(1) - 1)
    def _():
        y = acc_ref[...]
        if final:
            y = _rms(y, gf_ref[...])
        o_ref[...] = y


def _mlp(x, gain, w1, w2, layer, final_gain=None, tm_pref=1024, tf_pref=1024):
    m, d = x.shape
    dff = w1.shape[2]
    tm = _tile(m, tm_pref)
    tf = _tile(dff, tf_pref)
    in_specs = [pl.BlockSpec((tm, d), lambda i, f: (i, 0)),
                _const_spec((1, d)),
                pl.BlockSpec((None, d, tf), lambda i, f: (layer, 0, f)),
                pl.BlockSpec((None, tf, d), lambda i, f: (layer, f, 0))]
    args = [x, gain.reshape(1, d), w1, w2]
    if final_gain is not None:
        in_specs.append(_const_spec((1, d)))
        args.append(final_gain.reshape(1, d))
    return pl.pallas_call(
        functools.partial(_mlp_kernel, final=final_gain is not None), name="mlp",
        grid=(m // tm, dff // tf), in_specs=in_specs,
        out_specs=pl.BlockSpec((tm, d), lambda i, f: (i, 0)),
        out_shape=jax.ShapeDtypeStruct((m, d), F32),
        scratch_shapes=[pltpu.VMEM((tm, d), BF16), pltpu.VMEM((tm, d), F32)],
        compiler_params=_cparams("parallel", "arbitrary"))(*args)


def _mlstm_front_kernel(xm_ref, halo_ref, conv0_ref, cw_ref, cb_ref, wq_ref, wk_ref, wv_ref, wg_ref,
                        gb_ref, q_ref, k_ref, v_ref, xc_ref, gcol_ref, grow_ref, xp_ref, *, tt):
    t = pl.program_id(1)
    xm = xm_ref[0]
    prev = jnp.where(t == 0, conv0_ref[0], halo_ref[0])
    xp_ref[0:8, :] = prev
    xp_ref[8:8 + tt, :] = xm
    xc = cb_ref[...] + xm * cw_ref[A_CONV - 1:A_CONV, :]
    for j in range(A_CONV - 1):
        xc = xc + xp_ref[5 + j:5 + j + tt, :] * cw_ref[j:j + 1, :]
    xc = xc / (1.0 + jnp.exp(-xc))
    xc_ref[0] = xc.astype(BF16)
    g = jnp.zeros((tt, LANES), F32)
    for h in range(A_HEADS):
        sl = slice(h * A_DH, (h + 1) * A_DH)
        xch = xc[:, sl].astype(BF16)
        xmh = xm[:, sl].astype(BF16)
        qh = jnp.dot(xch, wq_ref[h], preferred_element_type=F32).astype(BF16)
        kh = (jnp.dot(xch, wk_ref[h], preferred_element_type=F32) * (A_DH ** -0.5)).astype(BF16)
        vh = jnp.dot(xmh, wv_ref[h], preferred_element_type=F32).astype(BF16)
        q_ref[0, :, sl] = qh
        k_ref[0, :, sl] = kh
        v_ref[0, :, sl] = vh
        g = g + jnp.dot(qh, wg_ref[0, h], preferred_element_type=F32)
        g = g + jnp.dot(kh, wg_ref[1, h], preferred_element_type=F32)
        g = g + jnp.dot(vh, wg_ref[2, h], preferred_element_type=F32)
    g = g + gb_ref[...]
    lane = lax.broadcasted_iota(jnp.int32, (tt, LANES), 1)
    g = jnp.where(lane < A_HEADS, g, _log_sigmoid(g))
    gcol_ref[0] = g
    sel = (lax.broadcasted_iota(jnp.int32, (8, LANES), 0) ==
           lax.broadcasted_iota(jnp.int32, (8, LANES), 1)).astype(BF16)
    grow = jnp.zeros((8, tt), F32)
    rem = g
    for _ in range(3):
        part = rem.astype(BF16)
        grow = grow + lax.dot_general(sel, part, (((1,), (1,)), ((), ())), preferred_element_type=F32)
        rem = rem - part.astype(F32)
    grow_ref[0] = grow


def _mlstm_front(xm, conv0p, cw, cb, wq, wk, wv, wg, gb, layer, tt_pref=256):
    bsz, t_len, _ = xm.shape
    tt = _tile(t_len, tt_pref)
    nblk8 = tt // 8
    act = lambda dt: jax.ShapeDtypeStruct((bsz, t_len, A_INNER), dt)
    act_spec = pl.BlockSpec((1, tt, A_INNER), lambda b, t: (b, t, 0))
    return pl.pallas_call(
        functools.partial(_mlstm_front_kernel, tt=tt), name="mlstm_front",
        grid=(bsz, t_len // tt),
        in_specs=[act_spec,
                  pl.BlockSpec((1, 8, A_INNER), lambda b, t: (b, jnp.maximum(t * nblk8 - 1, 0), 0)),
                  pl.BlockSpec((1, 8, A_INNER), lambda b, t: (b, 0, 0)),
                  _const_spec(cw.shape), _const_spec(cb.shape), _const_spec(wq.shape, layer),
                  _const_spec(wk.shape, layer), _const_spec(wv.shape, layer), _const_spec(wg.shape, layer),
                  _const_spec(gb.shape)],
        out_specs=[act_spec, act_spec, act_spec, act_spec,
                   pl.BlockSpec((1, tt, LANES), lambda b, t: (b, t, 0)),
                   pl.BlockSpec((1, 8, tt), lambda b, t: (b, 0, t))],
        out_shape=[act(BF16), act(BF16), act(BF16), act(BF16),
                   jax.ShapeDtypeStruct((bsz, t_len, LANES), F32),
                   jax.ShapeDtypeStruct((bsz, 8, t_len), F32)],
        scratch_shapes=[pltpu.VMEM((tt + 8, A_INNER), F32)],
        compiler_params=_cparams("parallel", "arbitrary"))(xm, xm, conv0p, cw, cb, wq, wk, wv, wg, gb)


def _mlstm_rec_kernel(*refs, lc, has_state):
    if has_state:
        (q_ref, k_ref, v_ref, gcol_ref, grow_ref, gh_ref, c0_ref, n0_ref, m0_ref,
         hn_ref, c_ref, n_ref, m_ref) = refs
    else:
        q_ref, k_ref, v_ref, gcol_ref, grow_ref, gh_ref, hn_ref, c_ref, n_ref, m_ref = refs
    c_idx = pl.program_id(1)

    @pl.when(c_idx == 0)
    def _():
        if has_state:
            c_ref[...] = c0_ref[...]
            n_ref[...] = n0_ref[...]
            m_ref[...] = m0_ref[...]
        else:
            c_ref[...] = jnp.zeros_like(c_ref)
            n_ref[...] = jnp.zeros_like(n_ref)
            m_ref[...] = jnp.zeros_like(m_ref)

    row = lax.broadcasted_iota(jnp.int32, (lc, lc), 0)
    col = lax.broadcasted_iota(jnp.int32, (lc, lc), 1)
    causal = col <= row
    gcol = gcol_ref[0]
    grow = grow_ref[0]
    for h in range(A_HEADS):
        sl = slice(h * A_DH, (h + 1) * A_DH)
        qh = q_ref[0, :, sl]
        kh = k_ref[0, :, sl]
        vh = v_ref[0, :, sl]
        li_c = gcol[:, h:h + 1]
        lf_c = gcol[:, A_HEADS + h:A_HEADS + h + 1]
        li_r = grow[h:h + 1, :]
        lf_r = grow[A_HEADS + h:A_HEADS + h + 1, :]
        b_c = jnp.sum(jnp.where(causal, lf_r, 0.0), axis=1, keepdims=True)
        b_r = jnp.sum(jnp.where(row <= col, lf_c, 0.0), axis=0, keepdims=True)
        c_old = c_ref[0, h]
        n_old = n_ref[0, h:h + 1, :]
        m_old = m_ref[0, h:h + 1, 0:1]
        dmat = jnp.where(causal, b_c - b_r + li_r, NEG_BIG)
        inter = b_c + m_old
        m_t = jnp.maximum(inter, jnp.max(dmat, axis=1, keepdims=True))
        w_inter = jnp.exp(inter - m_t)
        s = lax.dot_general(qh, kh, (((1,), (1,)), ((), ())), preferred_element_type=F32)
        s = s * jnp.exp(dmat - m_t)
        num = w_inter * jnp.dot(qh, c_old.astype(BF16), preferred_element_type=F32)
        num = num + jnp.dot(s.astype(BF16), vh, preferred_element_type=F32)
        nq = w_inter * jnp.sum(qh.astype(F32) * n_old, axis=1, keepdims=True)
        nq = nq + jnp.sum(s, axis=1, keepdims=True)
        hv = num / jnp.maximum(jnp.abs(nq), jnp.exp(-m_t))
        mu = jnp.mean(hv, axis=1, keepdims=True)
        hc = hv - mu
        var = jnp.mean(hc * hc, axis=1, keepdims=True)
        hn_ref[0, :, sl] = (hc * lax.rsqrt(var + EPS) * gh_ref[:, sl]).astype(hn_ref.dtype)
        m_new = m_t[lc - 1:lc, :]
        b_last = b_c[lc - 1:lc, :]
        g_state = jnp.exp(b_last + m_old - m_new)
        g_tok = jnp.exp(b_last - b_c + li_c - m_new)
        kw = kh.astype(F32) * g_tok
        c_ref[0, h] = g_state * c_old + jnp.dot(kw.T.astype(BF16), vh, preferred_element_type=F32)
        n_ref[0, h:h + 1, :] = g_state * n_old + jnp.sum(kw, axis=0, keepdims=True)
        m_ref[0, h:h + 1, :] = jnp.broadcast_to(m_new, (1, LANES))


def _mlstm_rec(q, k, v, gcol, grow, g_head, state, layer, lc_pref=256):
    bsz, t_len, _ = q.shape
    lc = _tile(t_len, lc_pref)
    act_spec = pl.BlockSpec((1, lc, A_INNER), lambda b, c: (b, c, 0))
    st_shapes = [(1, A_HEADS, A_DH, A_DH), (1, A_HEADS, A_DH), (1, A_HEADS, LANES)]
    st_specs = [pl.BlockSpec(shp, lambda b, c, nd=len(shp): (b,) + (0,) * (nd - 1)) for shp in st_shapes]
    in_specs = [act_spec, act_spec, act_spec,
                pl.BlockSpec((1, lc, LANES), lambda b, c: (b, c, 0)),
                pl.BlockSpec((1, 8, lc), lambda b, c: (b, 0, c)),
                _const_spec((1, A_INNER))]
    args = [q, k, v, gcol, grow, g_head]
    if state is not None:
        in_specs += [pl.BlockSpec((None,) + shp, lambda b, c, nd=len(shp): (layer, b) + (0,) * (nd - 1))
                     for shp in st_shapes]
        args += list(state)
    return pl.pallas_call(
        functools.partial(_mlstm_rec_kernel, lc=lc, has_state=state is not None), name="mlstm_rec",
        grid=(bsz, t_len // lc), in_specs=in_specs,
        out_specs=[act_spec] + st_specs,
        out_shape=[jax.ShapeDtypeStruct((bsz, t_len, A_INNER), BF16),
                   jax.ShapeDtypeStruct((bsz, A_HEADS, A_DH, A_DH), F32),
                   jax.ShapeDtypeStruct((bsz, A_HEADS, A_DH), F32),
                   jax.ShapeDtypeStruct((bsz, A_HEADS, LANES), F32)],
        compiler_params=_cparams("parallel", "arbitrary"))(*args)


def _mlstm_out_kernel(hn_ref, xc_ref, z_ref, skip_ref, w_ref, x_ref, o_ref):
    z = z_ref[...].astype(F32)
    a = (hn_ref[...].astype(F32) + skip_ref[...] * xc_ref[...].astype(F32)) * (1.0 / (1.0 + jnp.exp(-z)))
    o_ref[...] = x_ref[...] + jnp.dot(a.astype(BF16), w_ref[...], preferred_element_type=F32)


def _mlstm_out(hn, xc, z, skip, w_down, layer, x, tm_pref=512):
    m, d = x.shape
    tm = _tile(m, tm_pref)
    row_spec = lambda n: pl.BlockSpec((tm, n), lambda i: (i, 0))
    return pl.pallas_call(
        _mlstm_out_kernel, name="mlstm_out", grid=(m // tm,),
        in_specs=[row_spec(A_INNER), row_spec(A_INNER), row_spec(A_INNER),
                  _const_spec((1, A_INNER)), _const_spec(w_down.shape, layer), row_spec(d)],
        out_specs=row_spec(d),
        out_shape=jax.ShapeDtypeStruct((m, d), F32),
        compiler_params=_cparams("parallel"))(hn, xc, z, skip, w_down, x)


def _neg_abs(x):
    bits = lax.bitcast_convert_type(x, jnp.uint32) | jnp.uint32(0x80000000)
    return lax.bitcast_convert_type(bits, F32)


def _sb_attn_kernel(q_ref, k_ref, v_ref, u_ref, o_ref, *, tq, tk, q_off):
    i = pl.program_id(2)
    nsub = tk // LANES
    q2 = q_ref[0]
    lane = lax.broadcasted_iota(jnp.int32, (tq, LANES), 1)
    qmin = q_off + i * tq
    qpos = qmin + lax.broadcasted_iota(jnp.int32, (tq, tk), 0)
    kidx = lax.broadcasted_iota(jnp.int32, (tq, tk), 1)
    nkb = (qmin + tq - 2) // tk + 1
    n_diag = nkb - qmin // tk
    u = u_ref[...]
    zeros = jnp.zeros((tq, LANES), F32)
    qms = [jnp.where((lane >= HALF * e) & (lane < HALF * (e + 1)), q2, jnp.zeros_like(q2)) for e in range(2)]

    def make_body(masked):
        def body(j, carry):
            kb = nkb - 1 - j
            start = pl.multiple_of(kb * tk, tk)
            k2 = k_ref[0, pl.ds(start, tk), :]
            v2 = v_ref[0, pl.ds(start, tk), :]
            before = (kidx + start) < qpos if masked else None
            new = []
            for e in range(2):
                acc, run = carry[2 * e], carry[2 * e + 1]
                z = lax.dot_general(qms[e], k2, (((1,), (1,)), ((), ())), preferred_element_type=F32)
                sp = jnp.maximum(z, 0.0) + jnp.log(1.0 + jnp.exp2(_neg_abs(z))) * LOG2E
                nlf = jnp.where(before, sp, 0.0) if masked else sp
                rests = [None] * nsub
                for sb in reversed(range(nsub)):
                    blk = nlf[:, sb * LANES:(sb + 1) * LANES]
                    hi = blk.astype(BF16)
                    lo = (blk - hi.astype(F32)).astype(BF16)
                    r = jnp.dot(jnp.concatenate([hi, lo], axis=1), u, preferred_element_type=F32)
                    rests[sb] = r[:, :LANES] + run
                    run = run + r[:, LANES:]
                rest = rests[0] if nsub == 1 else jnp.concatenate(rests, axis=1)
                att = jnp.exp2(z - sp + rest)
                if masked:
                    att = jnp.where(before, att, 0.0)
                acc = acc + jnp.dot(att.astype(BF16), v2, preferred_element_type=F32)
                new += [acc, run]
            return tuple(new)
        return body

    res = lax.fori_loop(0, n_diag, make_body(True), (zeros,) * 4)
    res = lax.fori_loop(n_diag, nkb, make_body(False), res)
    o_ref[0] = jnp.where(lane < HALF, res[0], res[2]).astype(o_ref.dtype)


def _key_tile(tk_len):
    return tk_len if tk_len <= 10 * LANES else 4 * LANES


def _sb_attn(q, k_all, v_all, u, q_off, tq_pref=512):
    bsz, tq_len, d = q.shape
    tk_len = k_all.shape[1]
    tq = _tile(tq_len, tq_pref)
    tk = _key_tile(tk_len)
    assert tk_len % tk == 0 and (q_off + tq_len - 2) // tk < tk_len // tk
    return pl.pallas_call(
        functools.partial(_sb_attn_kernel, tq=tq, tk=tk, q_off=q_off), name="sb_attn",
        grid=(bsz, d // LANES, tq_len // tq),
        in_specs=[pl.BlockSpec((1, tq, LANES), lambda b, h, i: (b, i, h)),
                  pl.BlockSpec((1, tk_len, LANES), lambda b, h, i: (b, 0, h)),
                  pl.BlockSpec((1, tk_len, LANES), lambda b, h, i: (b, 0, h)),
                  _const_spec(u.shape)],
        out_specs=pl.BlockSpec((1, tq, LANES), lambda b, h, i: (b, i, h)),
        out_shape=jax.ShapeDtypeStruct((bsz, tq_len, d), BF16),
        compiler_params=_cparams("parallel", "parallel", "arbitrary"))(q, k_all, v_all, u)


def _rope128(x, cos_t, sin_s, lane):
    rot = jnp.where(lane < HALF + C_ROPE // 2, pltpu.roll(x, LANES - C_ROPE // 2, 1),
                    pltpu.roll(x, C_ROPE // 2, 1))
    return x * cos_t + rot * sin_s


def _mla_mid_kernel(cq_ref, ckv_ref, kr_ref, gq_ref, gkv_ref, wuq_ref, cos_ref, sin_ref,
                    q_ref, ckvn_ref, krp_ref):
    tm = cq_ref.shape[0]
    lane = lax.broadcasted_iota(jnp.int32, (tm, LANES), 1)
    cos_t = cos_ref[...]
    sin_s = sin_ref[...]
    cq = _rms(cq_ref[...], gq_ref[...]).astype(BF16)
    for h in range(C_HEADS):
        sl = slice(h * LANES, (h + 1) * LANES)
        qh = jnp.dot(cq, wuq_ref[:, sl], preferred_element_type=F32)
        q_ref[:, sl] = (_rope128(qh, cos_t, sin_s, lane) * C_SCALE).astype(BF16)
    ckvn_ref[...] = _rms(ckv_ref[...], gkv_ref[...])
    krp_ref[...] = _rope128(kr_ref[...], cos_t, sin_s, lane)


def _mla_mid(cq_raw, ckv_raw, krp_raw, g_q, g_kv, wuq, layer, cos_t, sin_s, t_len, tm_pref=512):
    m = cq_raw.shape[0]
    tm = _tile(t_len, tm_pref)
    nt = t_len // tm
    rows = lambda n: pl.BlockSpec((tm, n), lambda i: (i, 0))
    tab = pl.BlockSpec((tm, LANES), lambda i: (i % nt, 0))
    return pl.pallas_call(
        _mla_mid_kernel, name="mla_mid", grid=(m // tm,),
        in_specs=[rows(cq_raw.shape[1]), rows(C_KV_RANK), rows(LANES),
                  _const_spec((1, cq_raw.shape[1])), _const_spec((1, C_KV_RANK)),
                  _const_spec(wuq.shape, layer), tab, tab],
        out_specs=[rows(C_HEADS * LANES), rows(C_KV_RANK), rows(LANES)],
        out_shape=[jax.ShapeDtypeStruct((m, C_HEADS * LANES), BF16),
                   jax.ShapeDtypeStruct((m, C_KV_RANK), F32),
                   jax.ShapeDtypeStruct((m, LANES), F32)],
        compiler_params=_cparams("parallel"))(cq_raw, ckv_raw, krp_raw, g_q, g_kv, wuq, cos_t, sin_s)


def _mla_attn_kernel(q_ref, kv_ref, kr_ref, o_ref, *, tq, tk, q_off, n_valid):
    i = pl.program_id(2)
    lane_k = lax.broadcasted_iota(jnp.int32, (tk, LANES), 1)
    lane_q = lax.broadcasted_iota(jnp.int32, (tq, LANES), 1)
    qpos = q_off + i * tq + lax.broadcasted_iota(jnp.int32, (tq, tk), 0)
    klim = jnp.minimum((qpos | (CHUNK - 1)) + 1, n_valid)
    kidx = lax.broadcasted_iota(jnp.int32, (tq, tk), 1)
    n_vis = ((q_off + i * tq + tq - 1) // CHUNK + 1) * CHUNK
    nkb = (jnp.minimum(n_vis, n_valid) + tk - 1) // tk
    qs = [q_ref[0, :, e * LANES:(e + 1) * LANES] for e in range(2)]

    def body(kb, carry):
        start = pl.multiple_of(kb * tk, tk)
        krb = kr_ref[0, pl.ds(start, tk), :]
        vis = (kidx + start) < klim
        new = []
        for e in range(2):
            m_run, l_run, acc = carry[3 * e:3 * e + 3]
            kvb = kv_ref[0, pl.ds(start, tk), e * LANES:(e + 1) * LANES]
            kf = jnp.where(lane_k < HALF, kvb, krb)
            s = lax.dot_general(qs[e], kf, (((1,), (1,)), ((), ())), preferred_element_type=F32)
            s = jnp.where(vis, s, NEG_BIG)
            m_new = jnp.maximum(m_run, jnp.max(s, axis=1, keepdims=True))
            alpha = jnp.exp(m_run - m_new)
            p = jnp.exp(s - m_new)
            l_new = alpha * l_run + jnp.sum(p, axis=1, keepdims=True)
            acc = alpha * acc + jnp.dot(p.astype(BF16), kvb, preferred_element_type=F32)
            new += [m_new, l_new, acc]
        return tuple(new)

    init = (jnp.full((tq, 1), NEG_BIG, F32), jnp.zeros((tq, 1), F32), jnp.zeros((tq, LANES), F32)) * 2
    res = lax.fori_loop(0, nkb, body, init)
    o0 = res[2] / res[1]
    o1 = res[5] / res[4]
    o_ref[0] = jnp.where(lane_q < HALF, pltpu.roll(o0, HALF, 1), o1).astype(o_ref.dtype)


def _mla_attn(q, kv, krp, q_off, n_valid, tq_pref=512):
    bsz, tq_len, _ = q.shape
    tk_len = kv.shape[1]
    tq = _tile(tq_len, tq_pref)
    tk = _key_tile(tk_len)
    assert CHUNK & (CHUNK - 1) == 0 and tk_len % tk == 0 and n_valid <= tk_len
    return pl.pallas_call(
        functools.partial(_mla_attn_kernel, tq=tq, tk=tk, q_off=q_off, n_valid=n_valid), name="mla_attn",
        grid=(bsz, C_HEADS // 2, tq_len // tq),
        in_specs=[pl.BlockSpec((1, tq, 2 * LANES), lambda b, h, i: (b, i, h)),
                  pl.BlockSpec((1, tk_len, 2 * LANES), lambda b, h, i: (b, 0, h)),
                  pl.BlockSpec((1, tk_len, LANES), lambda b, h, i: (b, 0, 0))],
        out_specs=pl.BlockSpec((1, tq, LANES), lambda b, h, i: (b, i, h)),
        out_shape=jax.ShapeDtypeStruct((bsz, tq_len, C_HEADS * C_VDIM), BF16),
        compiler_params=_cparams("parallel", "parallel", "arbitrary"))(q, kv, krp)


def _pad_rows(a, n):
    return jnp.pad(a, ((0, 0), (0, n - a.shape[1]), (0, 0)))


def _mlstm_layer(x, gain, state, conv0, w, j, bsz, t_len):
    xm, z = _proj(x, gain, [(w["up"], j)], [(0, 0, A_INNER, F32), (0, A_INNER, 2 * A_INNER, BF16)],
                  name="mlstm_up")
    xm3 = xm.reshape(bsz, t_len, A_INNER)
    conv0p = jnp.pad(conv0, ((0, 0), (8 - (A_CONV - 1), 0), (0, 0)))
    q, k, v, xc, gcol, grow = _mlstm_front(xm3, conv0p, w["conv_w"][j], w["conv_b"][j], w["wq"], w["wk"],
                                           w["wv"], w["wg"], w["gb"][j], j)
    hn, c_new, n_new, m_new = _mlstm_rec(q, k, v, gcol, grow, w["g_head"][j], state, j)
    x = _mlstm_out(hn.reshape(bsz * t_len, A_INNER), xc.reshape(bsz * t_len, A_INNER), z,
                   w["skip"][j], w["down"], j, x)
    assert t_len >= A_CONV - 1
    return x, (c_new, n_new, m_new[..., 0], xm3[:, t_len - (A_CONV - 1):])


def _sb_layer(x, gain, past, w, j, bsz, t_len, q_off):
    d = x.shape[1]
    q, k, kb, v, vb = _proj(x, gain, [(w["wqkv"], j)],
                            [(0, 0, d, BF16), (0, d, 2 * d, F32), (0, d, 2 * d, BF16),
                             (0, 2 * d, 3 * d, F32), (0, 2 * d, 3 * d, BF16)], name="sb_qkv")
    k3 = kb.reshape(bsz, t_len, d)
    v3 = vb.reshape(bsz, t_len, d)
    if past is not None:
        k3 = jnp.concatenate([past[0].reshape(bsz, -1, d).astype(BF16), k3], axis=1)
        v3 = jnp.concatenate([past[1].reshape(bsz, -1, d).astype(BF16), v3], axis=1)
    tk_pad = -(-k3.shape[1] // LANES) * LANES
    k3 = _pad_rows(k3, tk_pad)
    v3 = _pad_rows(v3, tk_pad)
    o = _sb_attn(q.reshape(bsz, t_len, d), k3, v3, w["u"], q_off)
    (x,) = _proj(o.reshape(bsz * t_len, d), None, [(w["wo"], j)], [(0, 0, d, F32)], residual=x, name="sb_out")
    return x, (k.reshape(bsz, t_len, B_HEADS, B_DH), v.reshape(bsz, t_len, B_HEADS, B_DH))


def _rope_tables(t_len, q_off):
    half = C_ROPE // 2
    inv = ROPE_THETA ** (-jnp.arange(half, dtype=F32) / half)
    ang = (jnp.arange(t_len, dtype=F32) + q_off)[:, None] * inv[None, :]
    cos, sin = jnp.cos(ang), jnp.sin(ang)
    ones = jnp.ones((t_len, HALF), F32)
    zeros = jnp.zeros((t_len, HALF), F32)
    tail = LANES - HALF - C_ROPE
    cos_t = jnp.concatenate([ones, cos, cos, ones[:, :tail]], axis=1)
    sin_s = jnp.concatenate([zeros, -sin, sin, zeros[:, :tail]], axis=1)
    return cos_t, sin_s


def _mla_layer(x, gain, past, w, j, bsz, t_len, q_off):
    d = x.shape[1]
    rq = w["dq"].shape[2]
    cq_raw, ckv_raw, krp_raw = _proj(
        x, gain, [(w["dq"], j), (w["dkv"], j)],
        [(0, 0, rq, F32), (1, 0, C_KV_RANK, F32), (1, C_KV_RANK, C_KV_RANK + LANES, F32)], name="mla_down")
    cos_t, sin_s = _rope_tables(t_len, q_off)
    q, ckv, krp = _mla_mid(cq_raw, ckv_raw, krp_raw, w["g_q"][j], w["g_kv"][j], w["uq"], j, cos_t, sin_s, t_len)
    ckv3 = ckv.reshape(bsz, t_len, C_KV_RANK)
    krp3 = krp.reshape(bsz, t_len, LANES)
    kr_new = krp3[:, :, HALF:HALF + C_ROPE]
    if past is not None:
        ckv_all = jnp.concatenate([past[0], ckv3], axis=1)
        kr_pad = jnp.pad(past[1], ((0, 0), (0, 0), (HALF, LANES - HALF - C_ROPE)))
        krp_all = jnp.concatenate([kr_pad, krp3], axis=1)
    else:
        ckv_all, krp_all = ckv3, krp3
    n_valid = ckv_all.shape[1]
    tk_pad = -(-n_valid // LANES) * LANES
    ckv_all = _pad_rows(ckv_all, tk_pad)
    krp_all = _pad_rows(krp_all, tk_pad).astype(BF16)
    (kv,) = _proj(ckv_all.reshape(bsz * tk_pad, C_KV_RANK), None, [(w["ukv"], j)],
                  [(0, 0, C_HEADS * LANES, BF16)], name="mla_kv_up")
    o = _mla_attn(q.reshape(bsz, t_len, C_HEADS * LANES), kv.reshape(bsz, tk_pad, C_HEADS * LANES),
                  krp_all, q_off, n_valid)
    (x,) = _proj(o.reshape(bsz * t_len, C_HEADS * C_VDIM), None, [(w["wo"], j)], [(0, 0, d, F32)],
                 residual=x, name="mla_out")
    return x, (ckv3, kr_new)


def _trunk(x3, q_off, st_a, past_b, past_c, wts):
    bsz, t_len, d = x3.shape
    x = x3.reshape(bsz * t_len, d)
    new_a, new_b, new_c = [], [], []
    if st_a is not None:
        m_rep = jnp.broadcast_to(st_a[2][..., None], st_a[2].shape + (LANES,))
        state = (st_a[0], st_a[1], m_rep)
    for i in range(DEPTH):
        j = i // 3
        if i % 3 == 0:
            conv0 = jnp.zeros((bsz, A_CONV - 1, A_INNER), F32) if st_a is None else st_a[3][j]
            x, st = _mlstm_layer(x, wts["norm_mix"][i], None if st_a is None else state, conv0,
                                 wts["a"], j, bsz, t_len)
            new_a.append(st)
        elif i % 3 == 1:
            past = None if past_b is None else (past_b[0][j], past_b[1][j])
            x, kv = _sb_layer(x, wts["norm_mix"][i], past, wts["b"], j, bsz, t_len, q_off)
            new_b.append(kv)
        else:
            past = None if past_c is None else (past_c[0][j], past_c[1][j])
            x, lat = _mla_layer(x, wts["norm_mix"][i], past, wts["c"], j, bsz, t_len, q_off)
            new_c.append(lat)
        final = wts["norm_final"] if i == DEPTH - 1 else None
        x = _mlp(x, wts["norm_mlp"][i], wts["ff1"], wts["ff2"], i, final_gain=final)
    stack = lambda items, idx: jnp.stack([it[idx] for it in items])
    return (x.reshape(bsz, t_len, d),
            stack(new_a, 0), stack(new_a, 1), stack(new_a, 2), stack(new_a, 3),
            stack(new_b, 0), stack(new_b, 1), stack(new_c, 0), stack(new_c, 1))


def _prep_weights(norm_mix, norm_mlp, norm_final, a_w_up, a_conv_w, a_conv_b, a_w_q, a_w_k, a_w_v,
                  a_w_gate, a_b_i, a_b_f, a_g_head, a_skip, a_w_down, b_w_qkv, b_w_o, c_w_dq, c_g_q,
                  c_w_uq, c_w_dkv, c_g_kv, c_w_ukv, c_w_o, w_ff1, w_ff2):
    bf = lambda a: a.astype(BF16)
    n_a = a_w_up.shape[0]
    wg = a_w_gate.reshape(n_a, 3, A_HEADS, A_DH, 2 * A_HEADS)
    wg = jnp.pad(wg, ((0, 0),) * 4 + ((0, LANES - 2 * A_HEADS),))
    gb = jnp.pad(jnp.concatenate([a_b_i, a_b_f], axis=1), ((0, 0), (0, LANES - 2 * A_HEADS)))
    a = dict(up=bf(a_w_up), conv_w=a_conv_w, conv_b=a_conv_b.reshape(n_a, 1, A_INNER), wq=bf(a_w_q),
             wk=bf(a_w_k), wv=bf(a_w_v), wg=bf(wg), gb=gb.reshape(n_a, 1, LANES),
             g_head=a_g_head.reshape(n_a, 1, A_INNER), skip=a_skip.reshape(n_a, 1, A_INNER), down=bf(a_w_down))
    d = b_w_o.shape[1]
    qscale = jnp.where(jnp.arange(3 * d) < d, (B_DH ** -0.5) * LOG2E, 1.0).astype(F32)
    jj = lax.broadcasted_iota(jnp.int32, (2 * LANES, 2 * LANES), 0) % LANES
    ss = lax.broadcasted_iota(jnp.int32, (2 * LANES, 2 * LANES), 1)
    b = dict(wqkv=bf(b_w_qkv * qscale), wo=bf(b_w_o), u=-((ss >= LANES) | (jj > ss)).astype(BF16))
    n_c = c_w_dq.shape[0]
    uq = c_w_uq.reshape(n_c, -1, C_HEADS, C_NOPE + C_ROPE)
    uq = jnp.pad(uq, ((0, 0),) * 3 + ((0, LANES - C_NOPE - C_ROPE),)).reshape(n_c, -1, C_HEADS * LANES)
    dkv = jnp.concatenate([c_w_dkv[..., :C_KV_RANK], jnp.zeros(c_w_dkv.shape[:2] + (HALF,), F32),
                           c_w_dkv[..., C_KV_RANK:],
                           jnp.zeros(c_w_dkv.shape[:2] + (LANES - HALF - C_ROPE,), F32)], axis=-1)
    c = dict(dq=bf(c_w_dq), dkv=bf(dkv), g_q=c_g_q.reshape(n_c, 1, -1), g_kv=c_g_kv.reshape(n_c, 1, -1),
             uq=bf(uq), ukv=bf(c_w_ukv.reshape(n_c, C_KV_RANK, C_HEADS * LANES)), wo=bf(c_w_o))
    return dict(norm_mix=norm_mix, norm_mlp=norm_mlp, norm_final=norm_final, a=a, b=b, c=c,
                ff1=bf(w_ff1), ff2=bf(w_ff2))


def kernel(x_prompt, x_sample, state_mlstm_C, state_mlstm_n, state_mlstm_m, state_mlstm_conv, cache_sb_k, cache_sb_v, cache_mla_ckv, cache_mla_krope, norm_mix, norm_mlp, norm_final, a_w_up, a_conv_w, a_conv_b, a_w_q, a_w_k, a_w_v, a_w_gate, a_b_i, a_b_f, a_g_head, a_skip, a_w_down, b_w_qkv, b_w_o, c_w_dq, c_g_q, c_w_uq, c_w_dkv, c_g_kv, c_w_ukv, c_w_o, w_ff1, w_ff2):
    wts = _prep_weights(norm_mix, norm_mlp, norm_final, a_w_up, a_conv_w, a_conv_b, a_w_q, a_w_k, a_w_v,
                        a_w_gate, a_b_i, a_b_f, a_g_head, a_skip, a_w_down, b_w_qkv, b_w_o, c_w_dq, c_g_q,
                        c_w_uq, c_w_dkv, c_g_kv, c_w_ukv, c_w_o, w_ff1, w_ff2)
    outs_p = _trunk(x_prompt, 0, None, None, None, wts)
    past_len = cache_sb_k.shape[2]
    outs_s = _trunk(x_sample, past_len,
                    (state_mlstm_C, state_mlstm_n, state_mlstm_m, state_mlstm_conv),
                    (cache_sb_k, cache_sb_v), (cache_mla_ckv, cache_mla_krope), wts)
    y_p, rest_p = outs_p[0], outs_p[1:]
    y_s, rest_s = outs_s[0], outs_s[1:]
    return (y_p, y_s) + tuple(rest_p) + tuple(rest_s)
```

```python
import functools

import jax
import jax.numpy as jnp
from jax import lax
from jax.experimental import pallas as pl
from jax.experimental.pallas import tpu as pltpu

F32 = jnp.float32
BF16 = jnp.bfloat16

EPS = 1e-6
DEPTH = 4
CHUNK = 64
A_HEADS = 4
A_DH = 512
A_INNER = A_HEADS * A_DH
A_CONV = 4
B_HEADS = 16
B_DH = 64
C_HEADS = 16
C_NOPE = 64
C_ROPE = 32
C_VDIM = 64
C_KV_RANK = 256
C_SCALE = (C_NOPE + C_ROPE) ** -0.5
ROPE_THETA = 10000.0

LANES = 128
HALF = LANES // 2
NEG_BIG = -1e30
LOG2E = 1.4426950408889634
ATTN_GROUPS = 1
VMEM_LIMIT_BYTES = 56 * 1024 * 1024


def _cparams(*sem):
    return pltpu.CompilerParams(dimension_semantics=sem, vmem_limit_bytes=VMEM_LIMIT_BYTES)


def _tile(n, pref):
    if n <= pref:
        return n
    t = pref
    while n % t:
        t //= 2
    return t


def _const_spec(shape, layer=None):
    nd = len(shape)
    if layer is None:
        return pl.BlockSpec(shape, lambda *_: (0,) * nd)
    return pl.BlockSpec((None,) + tuple(shape[1:]), lambda *_: (layer,) + (0,) * (nd - 1))


def _rms(x, g):
    return x * lax.rsqrt(jnp.mean(x * x, axis=-1, keepdims=True) + EPS) * g


def _log_sigmoid(x):
    return jnp.minimum(x, 0.0) - jnp.log1p(jnp.exp(-jnp.abs(x)))


def _proj_kernel(*refs, n_w, outs, has_norm, has_res):
    refs = list(refs)
    x_ref = refs.pop(0)
    g_ref = refs.pop(0) if has_norm else None
    w_refs = [refs.pop(0) for _ in range(n_w)]
    r_ref = refs.pop(0) if has_res else None
    if has_norm:
        h = _rms(x_ref[...].astype(F32), g_ref[...]).astype(BF16)
    else:
        h = x_ref[...].astype(BF16)
    groups = {}
    for o_ref, (wi, c0, c1, _) in zip(refs, outs):
        groups.setdefault((wi, c0, c1), []).append(o_ref)
    for gi, ((wi, c0, c1), dests) in enumerate(groups.items()):
        n = c1 - c0
        c = n if n <= 512 else 512
        for n0 in range(0, n, c):
            y = jnp.dot(h, w_refs[wi][:, c0 + n0:c0 + n0 + c], preferred_element_type=F32)
            if has_res and gi == 0:
                y = y + r_ref[:, n0:n0 + c]
            for o_ref in dests:
                o_ref[:, n0:n0 + c] = y.astype(o_ref.dtype)


def _proj(x, gain, weights, outs, residual=None, tm_pref=512, name="proj"):
    m, k = x.shape
    tm = _tile(m, tm_pref)
    in_specs = [pl.BlockSpec((tm, k), lambda i: (i, 0))]
    args = [x]
    if gain is not None:
        in_specs.append(_const_spec((1, k)))
        args.append(gain.reshape(1, k).astype(F32))
    for w, layer in weights:
        in_specs.append(_const_spec(w.shape, layer))
        args.append(w)
    if residual is not None:
        in_specs.append(pl.BlockSpec((tm, residual.shape[1]), lambda i: (i, 0)))
        args.append(residual)
    out_shape = [jax.ShapeDtypeStruct((m, c1 - c0), dt) for _, c0, c1, dt in outs]
    out_specs = [pl.BlockSpec((tm, c1 - c0), lambda i: (i, 0)) for _, c0, c1, _ in outs]
    kern = functools.partial(_proj_kernel, n_w=len(weights), outs=tuple(outs),
                             has_norm=gain is not None, has_res=residual is not None)
    return pl.pallas_call(kern, name=name, grid=(m // tm,), in_specs=in_specs, out_specs=out_specs,
                          out_shape=out_shape, compiler_params=_cparams("parallel"))(*args)


def _mlp_kernel(*refs, final, tf):
    if final:
        x_ref, g_ref, w1_ref, w2_ref, gf_ref, o_ref = refs
    else:
        x_ref, g_ref, w1_ref, w2_ref, o_ref = refs
    x = x_ref[...]
    h = _rms(x, g_ref[...]).astype(BF16)
    acc = x
    for f0 in range(0, w1_ref.shape[1], tf):
        a = jnp.dot(h, w1_ref[:, f0:f0 + tf], preferred_element_type=F32)
        a = jnp.maximum(a, 0.0)
        a = a * a
        acc = acc + jnp.dot(a.astype(BF16), w2_ref[f0:f0 + tf, :], preferred_element_type=F32)
    if final:
        acc = _rms(acc, gf_ref[...])
    o_ref[...] = acc


def _mlp(x, gain, w1, w2, layer, final_gain=None, tm_pref=1024, tf_pref=1024):
    m, d = x.shape
    dff = w1.shape[2]
    tm = _tile(m, tm_pref)
    tf = _tile(dff, tf_pref)
    resident = lambda shp: pl.BlockSpec((None,) + shp, lambda i: (layer, 0, 0), pipeline_mode=pl.Buffered(1))
    in_specs = [pl.BlockSpec((tm, d), lambda i: (i, 0)), _const_spec((1, d)),
                resident((d, dff)), resident((dff, d))]
    args = [x, gain.reshape(1, d), w1, w2]
    if final_gain is not None:
        in_specs.append(_const_spec((1, d)))
        args.append(final_gain.reshape(1, d))
    return pl.pallas_call(
        functools.partial(_mlp_kernel, final=final_gain is not None, tf=tf), name="mlp",
        grid=(m // tm,), in_specs=in_specs,
        out_specs=pl.BlockSpec((tm, d), lambda i: (i, 0)),
        out_shape=jax.ShapeDtypeStruct((m, d), F32),
        compiler_params=_cparams("parallel"))(*args)


def _mlstm_front_kernel(xm_ref, halo_ref, conv0_ref, cw_ref, cb_ref, wq_ref, wk_ref, wv_ref, wg_ref,
                        gb_ref, q_ref, k_ref, v_ref, xc_ref, gcol_ref, grow_ref, xp_ref, *, tt):
    t = pl.program_id(1)
    xm = xm_ref[0]
    prev = jnp.where(t == 0, conv0_ref[0], halo_ref[0])
    xp_ref[0:8, :] = prev
    xp_ref[8:8 + tt, :] = xm
    g = jnp.zeros((tt, LANES), F32)
    for h in range(A_HEADS):
        sl = slice(h * A_DH, (h + 1) * A_DH)
        xc = cb_ref[:, sl] + xm[:, sl] * cw_ref[A_CONV - 1:A_CONV, sl]
        for j in range(A_CONV - 1):
            xc = xc + xp_ref[5 + j:5 + j + tt, sl] * cw_ref[j:j + 1, sl]
        xc = xc / (1.0 + jnp.exp(-xc))
        xch = xc.astype(BF16)
        xc_ref[0, :, sl] = xch
        xmh = xm[:, sl].astype(BF16)
        qh = jnp.dot(xch, wq_ref[h], preferred_element_type=F32).astype(BF16)
        kh = (jnp.dot(xch, wk_ref[h], preferred_element_type=F32) * (A_DH ** -0.5)).astype(BF16)
        vh = jnp.dot(xmh, wv_ref[h], preferred_element_type=F32).astype(BF16)
        q_ref[0, :, sl] = qh
        k_ref[0, :, sl] = kh
        v_ref[0, :, sl] = vh
        g = g + jnp.dot(qh, wg_ref[0, h], preferred_element_type=F32)
        g = g + jnp.dot(kh, wg_ref[1, h], preferred_element_type=F32)
        g = g + jnp.dot(vh, wg_ref[2, h], preferred_element_type=F32)
    g = g + gb_ref[...]
    lane = lax.broadcasted_iota(jnp.int32, (tt, LANES), 1)
    g = jnp.where(lane < A_HEADS, g, _log_sigmoid(g))
    gcol_ref[0] = g
    sel = (lax.broadcasted_iota(jnp.int32, (8, LANES), 0) ==
           lax.broadcasted_iota(jnp.int32, (8, LANES), 1)).astype(BF16)
    grow = jnp.zeros((8, tt), F32)
    rem = g
    for _ in range(3):
        part = rem.astype(BF16)
        grow = grow + lax.dot_general(sel, part, (((1,), (1,)), ((), ())), preferred_element_type=F32)
        rem = rem - part.astype(F32)
    grow_ref[0] = grow


def _mlstm_front(xm, conv0p, cw, cb, wq, wk, wv, wg, gb, layer, tt_pref=256):
    bsz, t_len, _ = xm.shape
    tt = _tile(t_len, tt_pref)
    nblk8 = tt // 8
    act = lambda dt: jax.ShapeDtypeStruct((bsz, t_len, A_INNER), dt)
    act_spec = pl.BlockSpec((1, tt, A_INNER), lambda b, t: (b, t, 0))
    return pl.pallas_call(
        functools.partial(_mlstm_front_kernel, tt=tt), name="mlstm_front",
        grid=(bsz, t_len // tt),
        in_specs=[act_spec,
                  pl.BlockSpec((1, 8, A_INNER), lambda b, t: (b, jnp.maximum(t * nblk8 - 1, 0), 0)),
                  pl.BlockSpec((1, 8, A_INNER), lambda b, t: (b, 0, 0)),
                  _const_spec(cw.shape), _const_spec(cb.shape), _const_spec(wq.shape, layer),
                  _const_spec(wk.shape, layer), _const_spec(wv.shape, layer), _const_spec(wg.shape, layer),
                  _const_spec(gb.shape)],
        out_specs=[act_spec, act_spec, act_spec, act_spec,
                   pl.BlockSpec((1, tt, LANES), lambda b, t: (b, t, 0)),
                   pl.BlockSpec((1, 8, tt), lambda b, t: (b, 0, t))],
        out_shape=[act(BF16), act(BF16), act(BF16), act(BF16),
                   jax.ShapeDtypeStruct((bsz, t_len, LANES), F32),
                   jax.ShapeDtypeStruct((bsz, 8, t_len), F32)],
        scratch_shapes=[pltpu.VMEM((tt + 8, A_INNER), F32)],
        compiler_params=_cparams("parallel", "arbitrary"))(xm, xm, conv0p, cw, cb, wq, wk, wv, wg, gb)


def _mlstm_rec_kernel(*refs, lc, has_state):
    if has_state:
        (q_ref, k_ref, v_ref, gcol_ref, grow_ref, gh_ref, c0_ref, n0_ref, m0_ref,
         hn_ref, c_ref, n_ref, m_ref) = refs
    else:
        q_ref, k_ref, v_ref, gcol_ref, grow_ref, gh_ref, hn_ref, c_ref, n_ref, m_ref = refs
    c_idx = pl.program_id(1)

    @pl.when(c_idx == 0)
    def _():
        if has_state:
            c_ref[...] = c0_ref[...]
            n_ref[...] = n0_ref[...]
            m_ref[...] = m0_ref[...]
        else:
            c_ref[...] = jnp.zeros_like(c_ref)
            n_ref[...] = jnp.zeros_like(n_ref)
            m_ref[...] = jnp.zeros_like(m_ref)

    row = lax.broadcasted_iota(jnp.int32, (lc, lc), 0)
    col = lax.broadcasted_iota(jnp.int32, (lc, lc), 1)
    causal = col <= row
    gcol = gcol_ref[0]
    grow = grow_ref[0]
    for h in range(A_HEADS):
        sl = slice(h * A_DH, (h + 1) * A_DH)
        qh = q_ref[0, :, sl]
        kh = k_ref[0, :, sl]
        vh = v_ref[0, :, sl]
        li_c = gcol[:, h:h + 1]
        lf_c = gcol[:, A_HEADS + h:A_HEADS + h + 1]
        li_r = grow[h:h + 1, :]
        lf_r = grow[A_HEADS + h:A_HEADS + h + 1, :]
        b_c = jnp.sum(jnp.where(causal, lf_r, 0.0), axis=1, keepdims=True)
        b_r = jnp.sum(jnp.where(row <= col, lf_c, 0.0), axis=0, keepdims=True)
        c_old = c_ref[0, h]
        n_old = n_ref[0, h:h + 1, :]
        m_old = m_ref[0, h:h + 1, 0:1]
        dmat = jnp.where(causal, b_c - b_r + li_r, NEG_BIG)
        inter = b_c + m_old
        m_t = jnp.maximum(inter, jnp.max(dmat, axis=1, keepdims=True))
        w_inter = jnp.exp(inter - m_t)
        s = lax.dot_general(qh, kh, (((1,), (1,)), ((), ())), preferred_element_type=F32)
        s = s * jnp.exp(dmat - m_t)
        num = w_inter * jnp.dot(qh, c_old.astype(BF16), preferred_element_type=F32)
        num = num + jnp.dot(s.astype(BF16), vh, preferred_element_type=F32)
        nq = w_inter * jnp.sum(qh.astype(F32) * n_old, axis=1, keepdims=True)
        nq = nq + jnp.sum(s, axis=1, keepdims=True)
        hv = num / jnp.maximum(jnp.abs(nq), jnp.exp(-m_t))
        mu = jnp.mean(hv, axis=1, keepdims=True)
        hc = hv - mu
        var = jnp.mean(hc * hc, axis=1, keepdims=True)
        hn_ref[0, :, sl] = (hc * lax.rsqrt(var + EPS) * gh_ref[:, sl]).astype(hn_ref.dtype)
        m_new = m_t[lc - 1:lc, :]
        b_last = b_c[lc - 1:lc, :]
        g_state = jnp.exp(b_last + m_old - m_new)
        g_tok = jnp.exp(b_last - b_c + li_c - m_new)
        kw = kh.astype(F32) * g_tok
        c_ref[0, h] = g_state * c_old + jnp.dot(kw.T.astype(BF16), vh, preferred_element_type=F32)
        n_ref[0, h:h + 1, :] = g_state * n_old + jnp.sum(kw, axis=0, keepdims=True)
        m_ref[0, h:h + 1, :] = jnp.broadcast_to(m_new, (1, LANES))


def _mlstm_rec(q, k, v, gcol, grow, g_head, state, layer, lc_pref=512):
    bsz, t_len, _ = q.shape
    lc = _tile(t_len, lc_pref)
    act_spec = pl.BlockSpec((1, lc, A_INNER), lambda b, c: (b, c, 0))
    st_shapes = [(1, A_HEADS, A_DH, A_DH), (1, A_HEADS, A_DH), (1, A_HEADS, LANES)]
    st_specs = [pl.BlockSpec(shp, lambda b, c, nd=len(shp): (b,) + (0,) * (nd - 1)) for shp in st_shapes]
    in_specs = [act_spec, act_spec, act_spec,
                pl.BlockSpec((1, lc, LANES), lambda b, c: (b, c, 0)),
                pl.BlockSpec((1, 8, lc), lambda b, c: (b, 0, c)),
                _const_spec((1, A_INNER))]
    args = [q, k, v, gcol, grow, g_head]
    if state is not None:
        in_specs += [pl.BlockSpec((None,) + shp, lambda b, c, nd=len(shp): (layer, b) + (0,) * (nd - 1))
                     for shp in st_shapes]
        args += list(state)
    return pl.pallas_call(
        functools.partial(_mlstm_rec_kernel, lc=lc, has_state=state is not None), name="mlstm_rec",
        grid=(bsz, t_len // lc), in_specs=in_specs,
        out_specs=[act_spec] + st_specs,
        out_shape=[jax.ShapeDtypeStruct((bsz, t_len, A_INNER), BF16),
                   jax.ShapeDtypeStruct((bsz, A_HEADS, A_DH, A_DH), F32),
                   jax.ShapeDtypeStruct((bsz, A_HEADS, A_DH), F32),
                   jax.ShapeDtypeStruct((bsz, A_HEADS, LANES), F32)],
        compiler_params=_cparams("parallel", "arbitrary"))(*args)


def _mlstm_out_kernel(hn_ref, xc_ref, z_ref, skip_ref, w_ref, x_ref, o_ref):
    z = z_ref[...].astype(F32)
    a = (hn_ref[...].astype(F32) + skip_ref[...] * xc_ref[...].astype(F32)) * (1.0 / (1.0 + jnp.exp(-z)))
    o_ref[...] = x_ref[...] + jnp.dot(a.astype(BF16), w_ref[...], preferred_element_type=F32)


def _mlstm_out(hn, xc, z, skip, w_down, layer, x, tm_pref=512):
    m, d = x.shape
    tm = _tile(m, tm_pref)
    row_spec = lambda n: pl.BlockSpec((tm, n), lambda i: (i, 0))
    return pl.pallas_call(
        _mlstm_out_kernel, name="mlstm_out", grid=(m // tm,),
        in_specs=[row_spec(A_INNER), row_spec(A_INNER), row_spec(A_INNER),
                  _const_spec((1, A_INNER)), _const_spec(w_down.shape, layer), row_spec(d)],
        out_specs=row_spec(d),
        out_shape=jax.ShapeDtypeStruct((m, d), F32),
        compiler_params=_cparams("parallel"))(hn, xc, z, skip, w_down, x)


def _sb_attn_kernel(q_ref, k_ref, v_ref, u_ref, o_ref, *, tq, tk, q_off, groups):
    i = pl.program_id(2)
    lane = lax.broadcasted_iota(jnp.int32, (tq, LANES), 1)
    qmin = q_off + i * tq
    nkb = (qmin + tq - 2) // tk + 1
    n_diag = nkb - qmin // tk
    u = u_ref[...]
    qms = []
    for g in range(groups):
        q2 = q_ref[0, :, g * LANES:(g + 1) * LANES]
        qms += [jnp.where((lane >= HALF * e) & (lane < HALF * (e + 1)), q2, jnp.zeros_like(q2)) for e in range(2)]

    def block(row0, nrows, start, nkeys, carry, masked):
        nsub = nkeys // LANES
        if masked:
            qpos = qmin + row0 + lax.broadcasted_iota(jnp.int32, (nrows, nkeys), 0)
            before = (lax.broadcasted_iota(jnp.int32, (nrows, nkeys), 1) + start) < qpos
        new = []
        for hd in range(2 * groups):
            sl = slice((hd // 2) * LANES, (hd // 2 + 1) * LANES)
            k2 = k_ref[0, pl.ds(start, nkeys), sl]
            v2 = v_ref[0, pl.ds(start, nkeys), sl]
            acc, run = carry[2 * hd], carry[2 * hd + 1]
            z = lax.dot_general(qms[hd][row0:row0 + nrows], k2, (((1,), (1,)), ((), ())),
                                preferred_element_type=F32)
            sp = jnp.maximum(z, 0.0) + jnp.log(1.0 + jnp.exp2(-jnp.abs(z))) * LOG2E
            nlf = jnp.where(before, sp, 0.0) if masked else sp
            rests = [None] * nsub
            for sb in reversed(range(nsub)):
                blk = nlf[:, sb * LANES:(sb + 1) * LANES]
                hi = blk.astype(BF16)
                lo = (blk - hi.astype(F32)).astype(BF16)
                r = jnp.dot(jnp.concatenate([hi, lo], axis=1), u, preferred_element_type=F32)
                rests[sb] = r[:, :LANES] + run
                run = run + r[:, LANES:]
            rest = rests[0] if nsub == 1 else jnp.concatenate(rests, axis=1)
            att = jnp.exp2(z - sp + rest)
            if masked:
                att = jnp.where(before, att, 0.0)
            acc = acc + jnp.dot(att.astype(BF16), v2, preferred_element_type=F32)
            new += [acc, run]
        return tuple(new)

    def full_block(masked):
        def body(j, carry):
            kb = nkb - 1 - j
            return block(0, tq, pl.multiple_of(kb * tk, tk), tk, carry, masked)
        return body

    n_carry = 4 * groups
    half = tq // 2
    if tq == tk and q_off % tk == 0 and half % LANES == 0:
        dstart = pl.multiple_of((nkb - 1) * tk, tk)
        zeros = jnp.zeros((half, LANES), F32)
        top = block(0, half, dstart, half, (zeros,) * n_carry, True)
        bot = block(half, half, dstart, tk, (zeros,) * n_carry, True)
        res = tuple(jnp.concatenate([a, b], axis=0) for a, b in zip(top, bot))
        res = lax.fori_loop(1, nkb, full_block(False), res)
    else:
        zeros = jnp.zeros((tq, LANES), F32)
        res = lax.fori_loop(0, n_diag, full_block(True), (zeros,) * n_carry)
        res = lax.fori_loop(n_diag, nkb, full_block(False), res)
    for g in range(groups):
        o_ref[0, :, g * LANES:(g + 1) * LANES] = jnp.where(lane < HALF, res[4 * g], res[4 * g + 2]).astype(o_ref.dtype)


def _key_tile(tk_len):
    return tk_len if tk_len <= 10 * LANES else 4 * LANES


def _sb_attn(q, k_all, v_all, u, q_off, tq_pref=512, groups=ATTN_GROUPS):
    bsz, tq_len, d = q.shape
    tk_len = k_all.shape[1]
    tq = _tile(tq_len, tq_pref)
    tk = _key_tile(tk_len)
    gw = groups * LANES
    assert tk_len % tk == 0 and (q_off + tq_len - 2) // tk < tk_len // tk and d % gw == 0
    return pl.pallas_call(
        functools.partial(_sb_attn_kernel, tq=tq, tk=tk, q_off=q_off, groups=groups), name="sb_attn",
        grid=(bsz, d // gw, tq_len // tq),
        in_specs=[pl.BlockSpec((1, tq, gw), lambda b, h, i: (b, i, h)),
                  pl.BlockSpec((1, tk_len, gw), lambda b, h, i: (b, 0, h)),
                  pl.BlockSpec((1, tk_len, gw), lambda b, h, i: (b, 0, h)),
                  _const_spec(u.shape)],
        out_specs=pl.BlockSpec((1, tq, gw), lambda b, h, i: (b, i, h)),
        out_shape=jax.ShapeDtypeStruct((bsz, tq_len, d), BF16),
        compiler_params=_cparams("parallel", "parallel", "arbitrary"))(q, k_all, v_all, u)


def _rope128(x, cos_t, sin_s, lane):
    rot = jnp.where(lane < HALF + C_ROPE // 2, pltpu.roll(x, LANES - C_ROPE // 2, 1),
                    pltpu.roll(x, C_ROPE // 2, 1))
    return x * cos_t + rot * sin_s


def _mla_mid_kernel(cq_ref, ckv_ref, kr_ref, gq_ref, gkv_ref, wuq_ref, cos_ref, sin_ref,
                    q_ref, ckvn_ref, krp_ref):
    tm = cq_ref.shape[0]
    lane = lax.broadcasted_iota(jnp.int32, (tm, LANES), 1)
    cos_t = cos_ref[...]
    sin_s = sin_ref[...]
    cq = _rms(cq_ref[...], gq_ref[...]).astype(BF16)
    for h in range(C_HEADS):
        sl = slice(h * LANES, (h + 1) * LANES)
        qh = jnp.dot(cq, wuq_ref[:, sl], preferred_element_type=F32)
        q_ref[:, sl] = (_rope128(qh, cos_t, sin_s, lane) * (C_SCALE * LOG2E)).astype(BF16)
    ckvn_ref[...] = _rms(ckv_ref[...], gkv_ref[...])
    krp_ref[...] = _rope128(kr_ref[...], cos_t, sin_s, lane)


def _mla_mid(cq_raw, ckv_raw, krp_raw, g_q, g_kv, wuq, layer, cos_t, sin_s, t_len, tm_pref=512):
    m = cq_raw.shape[0]
    tm = _tile(t_len, tm_pref)
    nt = t_len // tm
    rows = lambda n: pl.BlockSpec((tm, n), lambda i: (i, 0))
    tab = pl.BlockSpec((tm, LANES), lambda i: (i % nt, 0))
    return pl.pallas_call(
        _mla_mid_kernel, name="mla_mid", grid=(m // tm,),
        in_specs=[rows(cq_raw.shape[1]), rows(C_KV_RANK), rows(LANES),
                  _const_spec((1, cq_raw.shape[1])), _const_spec((1, C_KV_RANK)),
                  _const_spec(wuq.shape, layer), tab, tab],
        out_specs=[rows(C_HEADS * LANES), rows(C_KV_RANK), rows(LANES)],
        out_shape=[jax.ShapeDtypeStruct((m, C_HEADS * LANES), BF16),
                   jax.ShapeDtypeStruct((m, C_KV_RANK), F32),
                   jax.ShapeDtypeStruct((m, LANES), F32)],
        compiler_params=_cparams("parallel"))(cq_raw, ckv_raw, krp_raw, g_q, g_kv, wuq, cos_t, sin_s)


def _mla_attn_kernel(q_ref, kv_ref, kr_ref, o_ref, *, tq, tk, q_off, n_valid, groups):
    i = pl.program_id(2)
    lane_q = lax.broadcasted_iota(jnp.int32, (tq, LANES), 1)
    qmin = q_off + i * tq
    n_vis = ((qmin + tq - 1) // CHUNK + 1) * CHUNK
    nkb = (jnp.minimum(n_vis, n_valid) + tk - 1) // tk
    n_full = jnp.minimum((qmin | (CHUNK - 1)) + 1, n_valid) // tk
    nh = 2 * groups
    qs = [q_ref[0, :, hd * LANES:(hd + 1) * LANES] for hd in range(nh)]

    def block(row0, nrows, start, nkeys, carry, masked):
        krb = kr_ref[0, pl.ds(start, nkeys), :]
        lane_k = lax.broadcasted_iota(jnp.int32, (nkeys, LANES), 1)
        if masked:
            qpos = qmin + row0 + lax.broadcasted_iota(jnp.int32, (nrows, nkeys), 0)
            klim = jnp.minimum((qpos | (CHUNK - 1)) + 1, n_valid)
            vis = (lax.broadcasted_iota(jnp.int32, (nrows, nkeys), 1) + start) < klim
        new = []
        for hd in range(nh):
            m_run, l_run, acc = carry[3 * hd:3 * hd + 3]
            kvb = kv_ref[0, pl.ds(start, nkeys), hd * LANES:(hd + 1) * LANES]
            kf = jnp.where(lane_k < HALF, kvb, krb)
            s = lax.dot_general(qs[hd][row0:row0 + nrows], kf, (((1,), (1,)), ((), ())),
                                preferred_element_type=F32)
            if masked:
                s = jnp.where(vis, s, NEG_BIG)
            m_new = jnp.maximum(m_run, jnp.max(s, axis=1, keepdims=True))
            alpha = jnp.exp2(m_run - m_new)
            p = jnp.exp2(s - m_new)
            l_new = alpha * l_run + jnp.sum(p, axis=1, keepdims=True)
            acc = alpha * acc + jnp.dot(p.astype(BF16), kvb, preferred_element_type=F32)
            new += [m_new, l_new, acc]
        return tuple(new)

    def full_block(masked):
        def body(kb, carry):
            return block(0, tq, pl.multiple_of(kb * tk, tk), tk, carry, masked)
        return body

    init = (jnp.full((tq, 1), NEG_BIG, F32), jnp.zeros((tq, 1), F32), jnp.zeros((tq, LANES), F32)) * nh
    res = lax.fori_loop(0, n_full, full_block(False), init)
    half = tq // 2
    if tq == tk and q_off % tk == 0 and half % CHUNK == 0 and half % LANES == 0 and n_valid % tk == 0:
        dstart = pl.multiple_of((nkb - 1) * tk, tk)
        top = block(0, half, dstart, half, tuple(a[:half] for a in res), True)
        bot = block(half, half, dstart, tk, tuple(a[half:] for a in res), True)
        res = tuple(jnp.concatenate([a, b], axis=0) for a, b in zip(top, bot))
    else:
        res = lax.fori_loop(n_full, nkb, full_block(True), res)
    for g in range(groups):
        o0 = res[6 * g + 2] / res[6 * g + 1]
        o1 = res[6 * g + 5] / res[6 * g + 4]
        o = jnp.where(lane_q < HALF, pltpu.roll(o0, HALF, 1), o1)
        o_ref[0, :, g * LANES:(g + 1) * LANES] = o.astype(o_ref.dtype)


def _mla_attn(q, kv, krp, q_off, n_valid, tq_pref=512, groups=ATTN_GROUPS):
    bsz, tq_len, _ = q.shape
    tk_len = kv.shape[1]
    tq = _tile(tq_len, tq_pref)
    tk = _key_tile(tk_len)
    assert CHUNK & (CHUNK - 1) == 0 and tk_len % tk == 0 and n_valid <= tk_len and C_HEADS % (2 * groups) == 0
    return pl.pallas_call(
        functools.partial(_mla_attn_kernel, tq=tq, tk=tk, q_off=q_off, n_valid=n_valid, groups=groups),
        name="mla_attn",
        grid=(bsz, C_HEADS // (2 * groups), tq_len // tq),
        in_specs=[pl.BlockSpec((1, tq, 2 * groups * LANES), lambda b, h, i: (b, i, h)),
                  pl.BlockSpec((1, tk_len, 2 * groups * LANES), lambda b, h, i: (b, 0, h)),
                  pl.BlockSpec((1, tk_len, LANES), lambda b, h, i: (b, 0, 0))],
        out_specs=pl.BlockSpec((1, tq, groups * LANES), lambda b, h, i: (b, i, h)),
        out_shape=jax.ShapeDtypeStruct((bsz, tq_len, C_HEADS * C_VDIM), BF16),
        compiler_params=_cparams("parallel", "parallel", "arbitrary"))(q, kv, krp)


def _pad_rows(a, n):
    return jnp.pad(a, ((0, 0), (0, n - a.shape[1]), (0, 0)))


def _mlstm_layer(x, gain, state, conv0, w, j, bsz, t_len):
    xm, z = _proj(x, gain, [(w["up"], j)], [(0, 0, A_INNER, F32), (0, A_INNER, 2 * A_INNER, BF16)],
                  name="mlstm_up")
    xm3 = xm.reshape(bsz, t_len, A_INNER)
    conv0p = jnp.pad(conv0, ((0, 0), (8 - (A_CONV - 1), 0), (0, 0)))
    q, k, v, xc, gcol, grow = _mlstm_front(xm3, conv0p, w["conv_w"][j], w["conv_b"][j], w["wq"], w["wk"],
                                           w["wv"], w["wg"], w["gb"][j], j)
    hn, c_new, n_new, m_new = _mlstm_rec(q, k, v, gcol, grow, w["g_head"][j], state, j)
    x = _mlstm_out(hn.reshape(bsz * t_len, A_INNER), xc.reshape(bsz * t_len, A_INNER), z,
                   w["skip"][j], w["down"], j, x)
    assert t_len >= A_CONV - 1
    return x, (c_new, n_new, m_new[..., 0], xm3[:, t_len - (A_CONV - 1):])


def _sb_layer(x, gain, past, w, j, bsz, t_len, q_off):
    d = x.shape[1]
    q, k, kb, v, vb = _proj(x, gain, [(w["wqkv"], j)],
                            [(0, 0, d, BF16), (0, d, 2 * d, F32), (0, d, 2 * d, BF16),
                             (0, 2 * d, 3 * d, F32), (0, 2 * d, 3 * d, BF16)], name="sb_qkv")
    k3 = kb.reshape(bsz, t_len, d)
    v3 = vb.reshape(bsz, t_len, d)
    if past is not None:
        k3 = jnp.concatenate([past[0].reshape(bsz, -1, d).astype(BF16), k3], axis=1)
        v3 = jnp.concatenate([past[1].reshape(bsz, -1, d).astype(BF16), v3], axis=1)
    tk_pad = -(-k3.shape[1] // LANES) * LANES
    k3 = _pad_rows(k3, tk_pad)
    v3 = _pad_rows(v3, tk_pad)
    o = _sb_attn(q.reshape(bsz, t_len, d), k3, v3, w["u"], q_off)
    (x,) = _proj(o.reshape(bsz * t_len, d), None, [(w["wo"], j)], [(0, 0, d, F32)], residual=x, name="sb_out")
    return x, (k.reshape(bsz, t_len, B_HEADS, B_DH), v.reshape(bsz, t_len, B_HEADS, B_DH))


def _rope_tables(t_len, q_off):
    half = C_ROPE // 2
    inv = ROPE_THETA ** (-jnp.arange(half, dtype=F32) / half)
    ang = (jnp.arange(t_len, dtype=F32) + q_off)[:, None] * inv[None, :]
    cos, sin = jnp.cos(ang), jnp.sin(ang)
    ones = jnp.ones((t_len, HALF), F32)
    zeros = jnp.zeros((t_len, HALF), F32)
    tail = LANES - HALF - C_ROPE
    cos_t = jnp.concatenate([ones, cos, cos, ones[:, :tail]], axis=1)
    sin_s = jnp.concatenate([zeros, -sin, sin, zeros[:, :tail]], axis=1)
    return cos_t, sin_s


def _mla_layer(x, gain, past, w, j, bsz, t_len, q_off):
    d = x.shape[1]
    rq = w["dq"].shape[2]
    cq_raw, ckv_raw, krp_raw = _proj(
        x, gain, [(w["dq"], j), (w["dkv"], j)],
        [(0, 0, rq, F32), (1, 0, C_KV_RANK, F32), (1, C_KV_RANK, C_KV_RANK + LANES, F32)], name="mla_down")
    cos_t, sin_s = _rope_tables(t_len, q_off)
    q, ckv, krp = _mla_mid(cq_raw, ckv_raw, krp_raw, w["g_q"][j], w["g_kv"][j], w["uq"], j, cos_t, sin_s, t_len)
    ckv3 = ckv.reshape(bsz, t_len, C_KV_RANK)
    krp3 = krp.reshape(bsz, t_len, LANES)
    kr_new = krp3[:, :, HALF:HALF + C_ROPE]
    if past is not None:
        ckv_all = jnp.concatenate([past[0], ckv3], axis=1)
        kr_pad = jnp.pad(past[1], ((0, 0), (0, 0), (HALF, LANES - HALF - C_ROPE)))
        krp_all = jnp.concatenate([kr_pad, krp3], axis=1)
    else:
        ckv_all, krp_all = ckv3, krp3
    n_valid = ckv_all.shape[1]
    tk_pad = -(-n_valid // LANES) * LANES
    ckv_all = _pad_rows(ckv_all, tk_pad)
    krp_all = _pad_rows(krp_all, tk_pad).astype(BF16)
    (kv,) = _proj(ckv_all.reshape(bsz * tk_pad, C_KV_RANK), None, [(w["ukv"], j)],
                  [(0, 0, C_HEADS * LANES, BF16)], name="mla_kv_up")
    o = _mla_attn(q.reshape(bsz, t_len, C_HEADS * LANES), kv.reshape(bsz, tk_pad, C_HEADS * LANES),
                  krp_all, q_off, n_valid)
    (x,) = _proj(o.reshape(bsz * t_len, C_HEADS * C_VDIM), None, [(w["wo"], j)], [(0, 0, d, F32)],
                 residual=x, name="mla_out")
    return x, (ckv3, kr_new)


def _trunk(x3, q_off, st_a, past_b, past_c, wts):
    bsz, t_len, d = x3.shape
    x = x3.reshape(bsz * t_len, d)
    new_a, new_b, new_c = [], [], []
    if st_a is not None:
        m_rep = jnp.broadcast_to(st_a[2][..., None], st_a[2].shape + (LANES,))
        state = (st_a[0], st_a[1], m_rep)
    for i in range(DEPTH):
        j = i // 3
        if i % 3 == 0:
            conv0 = jnp.zeros((bsz, A_CONV - 1, A_INNER), F32) if st_a is None else st_a[3][j]
            x, st = _mlstm_layer(x, wts["norm_mix"][i], None if st_a is None else state, conv0,
                                 wts["a"], j, bsz, t_len)
            new_a.append(st)
        elif i % 3 == 1:
            past = None if past_b is None else (past_b[0][j], past_b[1][j])
            x, kv = _sb_layer(x, wts["norm_mix"][i], past, wts["b"], j, bsz, t_len, q_off)
            new_b.append(kv)
        else:
            past = None if past_c is None else (past_c[0][j], past_c[1][j])
            x, lat = _mla_layer(x, wts["norm_mix"][i], past, wts["c"], j, bsz, t_len, q_off)
            new_c.append(lat)
        final = wts["norm_final"] if i == DEPTH - 1 else None
        x = _mlp(x, wts["norm_mlp"][i], wts["ff1"], wts["ff2"], i, final_gain=final)
    stack = lambda items, idx: jnp.stack([it[idx] for it in items])
    return (x.reshape(bsz, t_len, d),
            stack(new_a, 0), stack(new_a, 1), stack(new_a, 2), stack(new_a, 3),
            stack(new_b, 0), stack(new_b, 1), stack(new_c, 0), stack(new_c, 1))


def _prep_weights(norm_mix, norm_mlp, norm_final, a_w_up, a_conv_w, a_conv_b, a_w_q, a_w_k, a_w_v,
                  a_w_gate, a_b_i, a_b_f, a_g_head, a_skip, a_w_down, b_w_qkv, b_w_o, c_w_dq, c_g_q,
                  c_w_uq, c_w_dkv, c_g_kv, c_w_ukv, c_w_o, w_ff1, w_ff2):
    bf = lambda a: a.astype(BF16)
    n_a = a_w_up.shape[0]
    wg = a_w_gate.reshape(n_a, 3, A_HEADS, A_DH, 2 * A_HEADS)
    wg = jnp.pad(wg, ((0, 0),) * 4 + ((0, LANES - 2 * A_HEADS),))
    gb = jnp.pad(jnp.concatenate([a_b_i, a_b_f], axis=1), ((0, 0), (0, LANES - 2 * A_HEADS)))
    a = dict(up=bf(a_w_up), conv_w=a_conv_w, conv_b=a_conv_b.reshape(n_a, 1, A_INNER), wq=bf(a_w_q),
             wk=bf(a_w_k), wv=bf(a_w_v), wg=bf(wg), gb=gb.reshape(n_a, 1, LANES),
             g_head=a_g_head.reshape(n_a, 1, A_INNER), skip=a_skip.reshape(n_a, 1, A_INNER), down=bf(a_w_down))
    d = b_w_o.shape[1]
    qscale = jnp.where(jnp.arange(3 * d) < d, (B_DH ** -0.5) * LOG2E, 1.0).astype(F32)
    jj = lax.broadcasted_iota(jnp.int32, (2 * LANES, 2 * LANES), 0) % LANES
    ss = lax.broadcasted_iota(jnp.int32, (2 * LANES, 2 * LANES), 1)
    b = dict(wqkv=bf(b_w_qkv * qscale), wo=bf(b_w_o), u=-((ss >= LANES) | (jj > ss)).astype(BF16))
    n_c = c_w_dq.shape[0]
    uq = c_w_uq.reshape(n_c, -1, C_HEADS, C_NOPE + C_ROPE)
    uq = jnp.pad(uq, ((0, 0),) * 3 + ((0, LANES - C_NOPE - C_ROPE),)).reshape(n_c, -1, C_HEADS * LANES)
    dkv = jnp.concatenate([c_w_dkv[..., :C_KV_RANK], jnp.zeros(c_w_dkv.shape[:2] + (HALF,), F32),
                           c_w_dkv[..., C_KV_RANK:],
                           jnp.zeros(c_w_dkv.shape[:2] + (LANES - HALF - C_ROPE,), F32)], axis=-1)
    c = dict(dq=bf(c_w_dq), dkv=bf(dkv), g_q=c_g_q.reshape(n_c, 1, -1), g_kv=c_g_kv.reshape(n_c, 1, -1),
             uq=bf(uq), ukv=bf(c_w_ukv.reshape(n_c, C_KV_RANK, C_HEADS * LANES)), wo=bf(c_w_o))
    return dict(norm_mix=norm_mix, norm_mlp=norm_mlp, norm_final=norm_final, a=a, b=b, c=c,
                ff1=bf(w_ff1), ff2=bf(w_ff2))


def kernel(x_prompt, x_sample, state_mlstm_C, state_mlstm_n, state_mlstm_m, state_mlstm_conv, cache_sb_k, cache_sb_v, cache_mla_ckv, cache_mla_krope, norm_mix, norm_mlp, norm_final, a_w_up, a_conv_w, a_conv_b, a_w_q, a_w_k, a_w_v, a_w_gate, a_b_i, a_b_f, a_g_head, a_skip, a_w_down, b_w_qkv, b_w_o, c_w_dq, c_g_q, c_w_uq, c_w_dkv, c_g_kv, c_w_ukv, c_w_o, w_ff1, w_ff2):
    wts = _prep_weights(norm_mix, norm_mlp, norm_final, a_w_up, a_conv_w, a_conv_b, a_w_q, a_w_k, a_w_v,
                        a_w_gate, a_b_i, a_b_f, a_g_head, a_skip, a_w_down, b_w_qkv, b_w_o, c_w_dq, c_g_q,
                        c_w_uq, c_w_dkv, c_g_kv, c_w_ukv, c_w_o, w_ff1, w_ff2)
    outs_p = _trunk(x_prompt, 0, None, None, None, wts)
    past_len = cache_sb_k.shape[2]
    outs_s = _trunk(x_sample, past_len,
                    (state_mlstm_C, state_mlstm_n, state_mlstm_m, state_mlstm_conv),
                    (cache_sb_k, cache_sb_v), (cache_mla_ckv, cache_mla_krope), wts)
    y_p, rest_p = outs_p[0], outs_p[1:]
    y_s, rest_s = outs_s[0], outs_s[1:]
    return (y_p, y_s) + tuple(rest_p) + tuple(rest_s)
```

```python
import functools

import jax
import jax.numpy as jnp
from jax import lax
from jax.experimental import pallas as pl
from jax.experimental.pallas import tpu as pltpu

F32 = jnp.float32
BF16 = jnp.bfloat16

EPS = 1e-6
DEPTH = 4
CHUNK = 64
A_HEADS = 4
A_DH = 512
A_INNER = A_HEADS * A_DH
A_CONV = 4
B_HEADS = 16
B_DH = 64
C_HEADS = 16
C_NOPE = 64
C_ROPE = 32
C_VDIM = 64
C_KV_RANK = 256
C_SCALE = (C_NOPE + C_ROPE) ** -0.5
ROPE_THETA = 10000.0

LANES = 128
MXU_DIM = 256
HALF = LANES // 2
NEG_BIG = -1e30
LOG2E = 1.4426950408889634
ATTN_GROUPS = 1
VMEM_LIMIT_BYTES = 56 * 1024 * 1024


def _cparams(*sem):
    return pltpu.CompilerParams(dimension_semantics=sem, vmem_limit_bytes=VMEM_LIMIT_BYTES)


def _tile(n, pref):
    if n <= pref:
        return n
    t = pref
    while n % t:
        t //= 2
    return t


def _const_spec(shape, layer=None):
    nd = len(shape)
    if layer is None:
        return pl.BlockSpec(shape, lambda *_: (0,) * nd)
    return pl.BlockSpec((None,) + tuple(shape[1:]), lambda *_: (layer,) + (0,) * (nd - 1))


def _rms(x, g):
    return x * lax.rsqrt(jnp.mean(x * x, axis=-1, keepdims=True) + EPS) * g


def _log_sigmoid(x):
    return jnp.minimum(x, 0.0) - jnp.log1p(jnp.exp(-jnp.abs(x)))


def _proj_kernel(*refs, n_w, outs, has_norm, has_res):
    refs = list(refs)
    x_ref = refs.pop(0)
    g_ref = refs.pop(0) if has_norm else None
    w_refs = [refs.pop(0) for _ in range(n_w)]
    r_ref = refs.pop(0) if has_res else None
    if has_norm:
        h = _rms(x_ref[...].astype(F32), g_ref[...]).astype(BF16)
    else:
        h = x_ref[...].astype(BF16)
    groups = {}
    for o_ref, (wi, c0, c1, _) in zip(refs, outs):
        groups.setdefault((wi, c0, c1), []).append(o_ref)
    for gi, ((wi, c0, c1), dests) in enumerate(groups.items()):
        n = c1 - c0
        c = n if n <= 512 else 512
        for n0 in range(0, n, c):
            y = jnp.dot(h, w_refs[wi][:, c0 + n0:c0 + n0 + c], preferred_element_type=F32)
            if has_res and gi == 0:
                y = y + r_ref[:, n0:n0 + c]
            for o_ref in dests:
                o_ref[:, n0:n0 + c] = y.astype(o_ref.dtype)


def _proj(x, gain, weights, outs, residual=None, tm_pref=512, name="proj"):
    m, k = x.shape
    tm = _tile(m, tm_pref)
    in_specs = [pl.BlockSpec((tm, k), lambda i: (i, 0))]
    args = [x]
    if gain is not None:
        in_specs.append(_const_spec((1, k)))
        args.append(gain.reshape(1, k).astype(F32))
    for w, layer in weights:
        in_specs.append(_const_spec(w.shape, layer))
        args.append(w)
    if residual is not None:
        in_specs.append(pl.BlockSpec((tm, residual.shape[1]), lambda i: (i, 0)))
        args.append(residual)
    out_shape = [jax.ShapeDtypeStruct((m, c1 - c0), dt) for _, c0, c1, dt in outs]
    out_specs = [pl.BlockSpec((tm, c1 - c0), lambda i: (i, 0)) for _, c0, c1, _ in outs]
    kern = functools.partial(_proj_kernel, n_w=len(weights), outs=tuple(outs),
                             has_norm=gain is not None, has_res=residual is not None)
    return pl.pallas_call(kern, name=name, grid=(m // tm,), in_specs=in_specs, out_specs=out_specs,
                          out_shape=out_shape, compiler_params=_cparams("parallel"))(*args)


def _mlp_kernel(*refs, pre, final, tf):
    refs = list(refs)
    x_ref = refs.pop(0)
    if pre == "mlstm":
        hn_ref, xc_ref, z_ref, skip_ref = (refs.pop(0) for _ in range(4))
        z = z_ref[...].astype(F32)
        a = (hn_ref[...].astype(F32) + skip_ref[...] * xc_ref[...].astype(F32)) * (1.0 / (1.0 + jnp.exp(-z)))
        a = a.astype(BF16)
    else:
        a = refs.pop(0)[...]
    wo_ref, g_ref, w1_ref, w2_ref = (refs.pop(0) for _ in range(4))
    gf_ref = refs.pop(0) if final else None
    (o_ref,) = refs
    x = x_ref[...] + jnp.dot(a, wo_ref[...], preferred_element_type=F32)
    h = _rms(x, g_ref[...]).astype(BF16)
    acc = x
    for f0 in range(0, w1_ref.shape[1], tf):
        u = jnp.dot(h, w1_ref[:, f0:f0 + tf], preferred_element_type=F32)
        u = jnp.maximum(u, 0.0)
        u = u * u
        acc = acc + jnp.dot(u.astype(BF16), w2_ref[f0:f0 + tf, :], preferred_element_type=F32)
    if final:
        acc = _rms(acc, gf_ref[...])
    o_ref[...] = acc


def _mixer_out_mlp(x, mixed, wo, wo_layer, gain, w1, w2, layer, final_gain=None, tm_pref=512, tf_pref=1024):
    m, d = x.shape
    dff = w1.shape[2]
    tm = _tile(m, tm_pref)
    tf = _tile(dff, tf_pref)
    rows = lambda n: pl.BlockSpec((tm, n), lambda i: (i, 0))
    resident = lambda arr, lyr: pl.BlockSpec((None,) + arr.shape[1:], lambda i: (lyr, 0, 0),
                                             pipeline_mode=pl.Buffered(1))
    in_specs, args = [rows(d)], [x]
    if isinstance(mixed, tuple):
        hn, xc, z, skip = mixed
        pre = "mlstm"
        in_specs += [rows(hn.shape[1]), rows(xc.shape[1]), rows(z.shape[1]), _const_spec(skip.shape)]
        args += [hn, xc, z, skip]
    else:
        pre = "proj"
        in_specs.append(rows(mixed.shape[1]))
        args.append(mixed)
    in_specs += [resident(wo, wo_layer), _const_spec((1, d)), resident(w1, layer), resident(w2, layer)]
    args += [wo, gain.reshape(1, d), w1, w2]
    if final_gain is not None:
        in_specs.append(_const_spec((1, d)))
        args.append(final_gain.reshape(1, d))
    return pl.pallas_call(
        functools.partial(_mlp_kernel, pre=pre, final=final_gain is not None, tf=tf), name="out_mlp",
        grid=(m // tm,), in_specs=in_specs, out_specs=rows(d),
        out_shape=jax.ShapeDtypeStruct((m, d), F32),
        compiler_params=_cparams("parallel"))(*args)


def _mlstm_front_kernel(xm_ref, halo_ref, conv0_ref, cw_ref, cb_ref, wq_ref, wk_ref, wv_ref, wg_ref,
                        gb_ref, q_ref, k_ref, v_ref, xc_ref, gcol_ref, grow_ref, xp_ref, *, tt):
    t = pl.program_id(1)
    xm = xm_ref[0]
    prev = jnp.where(t == 0, conv0_ref[0], halo_ref[0])
    xp_ref[0:8, :] = prev
    xp_ref[8:8 + tt, :] = xm
    g = jnp.zeros((tt, LANES), F32)
    for h in range(A_HEADS):
        sl = slice(h * A_DH, (h + 1) * A_DH)
        xc = cb_ref[:, sl] + xm[:, sl] * cw_ref[A_CONV - 1:A_CONV, sl]
        for j in range(A_CONV - 1):
            xc = xc + xp_ref[5 + j:5 + j + tt, sl] * cw_ref[j:j + 1, sl]
        xc = xc / (1.0 + jnp.exp(-xc))
        xch = xc.astype(BF16)
        xc_ref[0, :, sl] = xch
        xmh = xm[:, sl].astype(BF16)
        qh = jnp.dot(xch, wq_ref[h], preferred_element_type=F32).astype(BF16)
        kh = (jnp.dot(xch, wk_ref[h], preferred_element_type=F32) * (A_DH ** -0.5)).astype(BF16)
        vh = jnp.dot(xmh, wv_ref[h], preferred_element_type=F32).astype(BF16)
        q_ref[0, :, sl] = qh
        k_ref[0, :, sl] = kh
        v_ref[0, :, sl] = vh
        g = g + jnp.dot(qh, wg_ref[0, h], preferred_element_type=F32)
        g = g + jnp.dot(kh, wg_ref[1, h], preferred_element_type=F32)
        g = g + jnp.dot(vh, wg_ref[2, h], preferred_element_type=F32)
    g = g + gb_ref[...]
    lane = lax.broadcasted_iota(jnp.int32, (tt, LANES), 1)
    g = jnp.where(lane < A_HEADS, g, _log_sigmoid(g))
    gcol_ref[0] = g
    sel = (lax.broadcasted_iota(jnp.int32, (8, LANES), 0) ==
           lax.broadcasted_iota(jnp.int32, (8, LANES), 1)).astype(BF16)
    grow = jnp.zeros((8, tt), F32)
    rem = g
    for _ in range(3):
        part = rem.astype(BF16)
        grow = grow + lax.dot_general(sel, part, (((1,), (1,)), ((), ())), preferred_element_type=F32)
        rem = rem - part.astype(F32)
    grow_ref[0] = grow


def _mlstm_front(xm, conv0p, cw, cb, wq, wk, wv, wg, gb, layer, tt_pref=256):
    bsz, t_len, _ = xm.shape
    tt = _tile(t_len, tt_pref)
    nblk8 = tt // 8
    act = lambda dt: jax.ShapeDtypeStruct((bsz, t_len, A_INNER), dt)
    act_spec = pl.BlockSpec((1, tt, A_INNER), lambda b, t: (b, t, 0))
    return pl.pallas_call(
        functools.partial(_mlstm_front_kernel, tt=tt), name="mlstm_front",
        grid=(bsz, t_len // tt),
        in_specs=[act_spec,
                  pl.BlockSpec((1, 8, A_INNER), lambda b, t: (b, jnp.maximum(t * nblk8 - 1, 0), 0)),
                  pl.BlockSpec((1, 8, A_INNER), lambda b, t: (b, 0, 0)),
                  _const_spec(cw.shape), _const_spec(cb.shape), _const_spec(wq.shape, layer),
                  _const_spec(wk.shape, layer), _const_spec(wv.shape, layer), _const_spec(wg.shape, layer),
                  _const_spec(gb.shape)],
        out_specs=[act_spec, act_spec, act_spec, act_spec,
                   pl.BlockSpec((1, tt, LANES), lambda b, t: (b, t, 0)),
                   pl.BlockSpec((1, 8, tt), lambda b, t: (b, 0, t))],
        out_shape=[act(BF16), act(BF16), act(BF16), act(BF16),
                   jax.ShapeDtypeStruct((bsz, t_len, LANES), F32),
                   jax.ShapeDtypeStruct((bsz, 8, t_len), F32)],
        scratch_shapes=[pltpu.VMEM((tt + 8, A_INNER), F32)],
        compiler_params=_cparams("parallel", "arbitrary"))(xm, xm, conv0p, cw, cb, wq, wk, wv, wg, gb)


def _mlstm_rec_kernel(*refs, lc, has_state, n_prev):
    refs = list(refs)
    q_ref, k_ref, v_ref, gcol_ref, grow_ref, gh_ref = (refs.pop(0) for _ in range(6))
    init_refs = [refs.pop(0) for _ in range(3)] if has_state else None
    prev_refs = [refs.pop(0) for _ in range(3)] if n_prev else None
    hn_ref, cs_ref, ns_ref, ms_ref = refs
    c_ref, n_ref, m_ref = cs_ref.at[n_prev], ns_ref.at[n_prev], ms_ref.at[n_prev]
    c_idx = pl.program_id(1)

    @pl.when(c_idx == 0)
    def _():
        for st_ref, i_ref in zip((c_ref, n_ref, m_ref), init_refs or (None,) * 3):
            st_ref[...] = jnp.zeros_like(st_ref) if i_ref is None else i_ref[...]
        if n_prev:
            for out_ref, p_ref in zip((cs_ref, ns_ref, ms_ref), prev_refs):
                out_ref[0:n_prev] = p_ref[...]

    row = lax.broadcasted_iota(jnp.int32, (lc, lc), 0)
    col = lax.broadcasted_iota(jnp.int32, (lc, lc), 1)
    causal = col <= row
    gcol = gcol_ref[0]
    grow = grow_ref[0]
    for h in range(A_HEADS):
        sl = slice(h * A_DH, (h + 1) * A_DH)
        qh = q_ref[0, :, sl]
        kh = k_ref[0, :, sl]
        vh = v_ref[0, :, sl]
        li_c = gcol[:, h:h + 1]
        lf_c = gcol[:, A_HEADS + h:A_HEADS + h + 1]
        li_r = grow[h:h + 1, :]
        lf_r = grow[A_HEADS + h:A_HEADS + h + 1, :]
        b_c = jnp.sum(jnp.where(causal, lf_r, 0.0), axis=1, keepdims=True)
        b_r = jnp.sum(jnp.where(row <= col, lf_c, 0.0), axis=0, keepdims=True)
        c_old = c_ref[0, h]
        n_old = n_ref[0, h:h + 1, :]
        m_old = m_ref[0, h:h + 1, 0:1]
        dmat = jnp.where(causal, b_c - b_r + li_r, NEG_BIG)
        inter = b_c + m_old
        m_t = jnp.maximum(inter, jnp.max(dmat, axis=1, keepdims=True))
        w_inter = jnp.exp(inter - m_t)
        s = lax.dot_general(qh, kh, (((1,), (1,)), ((), ())), preferred_element_type=F32)
        s = s * jnp.exp(dmat - m_t)
        num = w_inter * jnp.dot(qh, c_old.astype(BF16), preferred_element_type=F32)
        num = num + jnp.dot(s.astype(BF16), vh, preferred_element_type=F32)
        nq = w_inter * jnp.sum(qh.astype(F32) * n_old, axis=1, keepdims=True)
        nq = nq + jnp.sum(s, axis=1, keepdims=True)
        hv = num / jnp.maximum(jnp.abs(nq), jnp.exp(-m_t))
        mu = jnp.mean(hv, axis=1, keepdims=True)
        hc = hv - mu
        var = jnp.mean(hc * hc, axis=1, keepdims=True)
        hn_ref[0, :, sl] = (hc * lax.rsqrt(var + EPS) * gh_ref[:, sl]).astype(hn_ref.dtype)
        m_new = m_t[lc - 1:lc, :]
        b_last = b_c[lc - 1:lc, :]
        g_state = jnp.exp(b_last + m_old - m_new)
        g_tok = jnp.exp(b_last - b_c + li_c - m_new)
        kw = kh.astype(F32) * g_tok
        c_ref[0, h] = g_state * c_old + jnp.dot(kw.T.astype(BF16), vh, preferred_element_type=F32)
        n_ref[0, h:h + 1, :] = g_state * n_old + jnp.sum(kw, axis=0, keepdims=True)
        m_ref[0, h:h + 1, :] = jnp.broadcast_to(m_new, (1, LANES))


def _mlstm_rec(q, k, v, gcol, grow, g_head, state, layer, prev, lc_pref=256):
    bsz, t_len, _ = q.shape
    lc = _tile(t_len, lc_pref)
    n_prev = 0 if prev is None else prev[0].shape[0]
    act_spec = pl.BlockSpec((1, lc, A_INNER), lambda b, c: (b, c, 0))
    st_shapes = [(1, A_HEADS, A_DH, A_DH), (1, A_HEADS, A_DH), (1, A_HEADS, LANES)]
    stacked = lambda n: [pl.BlockSpec((n,) + shp, lambda b, c, nd=len(shp): (0, b) + (0,) * (nd - 1))
                         for shp in st_shapes]
    in_specs = [act_spec, act_spec, act_spec,
                pl.BlockSpec((1, lc, LANES), lambda b, c: (b, c, 0)),
                pl.BlockSpec((1, 8, lc), lambda b, c: (b, 0, c)),
                _const_spec((1, A_INNER))]
    args = [q, k, v, gcol, grow, g_head]
    if state is not None:
        in_specs += [pl.BlockSpec((None,) + shp, lambda b, c, nd=len(shp): (layer, b) + (0,) * (nd - 1))
                     for shp in st_shapes]
        args += list(state)
    if n_prev:
        in_specs += stacked(n_prev)
        args += list(prev)
    return pl.pallas_call(
        functools.partial(_mlstm_rec_kernel, lc=lc, has_state=state is not None, n_prev=n_prev),
        name="mlstm_rec", grid=(bsz, t_len // lc), in_specs=in_specs,
        out_specs=[act_spec] + stacked(n_prev + 1),
        out_shape=[jax.ShapeDtypeStruct((bsz, t_len, A_INNER), BF16)] +
                  [jax.ShapeDtypeStruct((n_prev + 1, bsz) + shp[1:], F32) for shp in st_shapes],
        compiler_params=_cparams("parallel", "arbitrary"))(*args)


def _sb_attn_kernel(q_ref, k_ref, v_ref, u_ref, o_ref, *, tq, tk, q_off, groups):
    i = pl.program_id(2)
    lane = lax.broadcasted_iota(jnp.int32, (tq, LANES), 1)
    qmin = q_off + i * tq
    nkb = (qmin + tq - 2) // tk + 1
    n_diag = nkb - qmin // tk
    u = u_ref[...]
    qms = []
    for g in range(groups):
        q2 = q_ref[0, :, g * LANES:(g + 1) * LANES]
        qms += [jnp.where((lane >= HALF * e) & (lane < HALF * (e + 1)), q2, jnp.zeros_like(q2)) for e in range(2)]

    def block(row0, nrows, start, nkeys, carry, masked):
        nsub = nkeys // LANES
        if masked:
            qpos = qmin + row0 + lax.broadcasted_iota(jnp.int32, (nrows, nkeys), 0)
            before = (lax.broadcasted_iota(jnp.int32, (nrows, nkeys), 1) + start) < qpos
        new = []
        for hd in range(2 * groups):
            sl = slice((hd // 2) * LANES, (hd // 2 + 1) * LANES)
            k2 = k_ref[0, pl.ds(start, nkeys), sl]
            v2 = v_ref[0, pl.ds(start, nkeys), sl]
            acc, run = carry[2 * hd], carry[2 * hd + 1]
            z = lax.dot_general(qms[hd][row0:row0 + nrows], k2, (((1,), (1,)), ((), ())),
                                preferred_element_type=F32)
            sp = jnp.maximum(z, 0.0) + jnp.log(1.0 + jnp.exp2(-jnp.abs(z))) * LOG2E
            nlf = jnp.where(before, sp, 0.0) if masked else sp
            hi = nlf.astype(BF16)
            lo = (nlf - hi.astype(F32)).astype(BF16)
            pieces = [jnp.concatenate([hi[:, sb * LANES:(sb + 1) * LANES], lo[:, sb * LANES:(sb + 1) * LANES]], axis=1)
                      for sb in range(nsub)]
            if nsub > 1 and nrows < MXU_DIM:
                r_all = jnp.dot(jnp.concatenate(pieces, axis=0), u, preferred_element_type=F32)
                rs = [r_all[sb * nrows:(sb + 1) * nrows] for sb in range(nsub)]
            else:
                rs = [jnp.dot(p, u, preferred_element_type=F32) for p in pieces]
            rests = [None] * nsub
            for sb in reversed(range(nsub)):
                rests[sb] = rs[sb][:, :LANES] + run
                run = run + rs[sb][:, LANES:]
            rest = rests[0] if nsub == 1 else jnp.concatenate(rests, axis=1)
            att = jnp.exp2(z - sp + rest)
            if masked:
                att = jnp.where(before, att, 0.0)
            acc = acc + jnp.dot(att.astype(BF16), v2, preferred_element_type=F32)
            new += [acc, run]
        return tuple(new)

    def full_block(masked):
        def body(j, carry):
            kb = nkb - 1 - j
            return block(0, tq, pl.multiple_of(kb * tk, tk), tk, carry, masked)
        return body

    n_carry = 4 * groups
    half = tq // 2
    if tq == tk and q_off % tk == 0 and half % LANES == 0:
        dstart = pl.multiple_of((nkb - 1) * tk, tk)
        zeros = jnp.zeros((half, LANES), F32)
        top = block(0, half, dstart, half, (zeros,) * n_carry, True)
        bot = block(half, half, dstart, tk, (zeros,) * n_carry, True)
        res = tuple(jnp.concatenate([a, b], axis=0) for a, b in zip(top, bot))
        res = lax.fori_loop(1, nkb, full_block(False), res)
    else:
        zeros = jnp.zeros((tq, LANES), F32)
        res = lax.fori_loop(0, n_diag, full_block(True), (zeros,) * n_carry)
        res = lax.fori_loop(n_diag, nkb, full_block(False), res)
    for g in range(groups):
        o_ref[0, :, g * LANES:(g + 1) * LANES] = jnp.where(lane < HALF, res[4 * g], res[4 * g + 2]).astype(o_ref.dtype)


def _key_tile(tk_len):
    return tk_len if tk_len <= 10 * LANES else 4 * LANES


def _sb_attn(q, k_all, v_all, u, q_off, tq_pref=512, groups=ATTN_GROUPS):
    bsz, tq_len, d = q.shape
    tk_len = k_all.shape[1]
    tq = _tile(tq_len, tq_pref)
    tk = _key_tile(tk_len)
    gw = groups * LANES
    assert tk_len % tk == 0 and (q_off + tq_len - 2) // tk < tk_len // tk and d % gw == 0
    return pl.pallas_call(
        functools.partial(_sb_attn_kernel, tq=tq, tk=tk, q_off=q_off, groups=groups), name="sb_attn",
        grid=(bsz, d // gw, tq_len // tq),
        in_specs=[pl.BlockSpec((1, tq, gw), lambda b, h, i: (b, i, h)),
                  pl.BlockSpec((1, tk_len, gw), lambda b, h, i: (b, 0, h)),
                  pl.BlockSpec((1, tk_len, gw), lambda b, h, i: (b, 0, h)),
                  _const_spec(u.shape)],
        out_specs=pl.BlockSpec((1, tq, gw), lambda b, h, i: (b, i, h)),
        out_shape=jax.ShapeDtypeStruct((bsz, tq_len, d), BF16),
        compiler_params=_cparams("parallel", "parallel", "arbitrary"))(q, k_all, v_all, u)


def _rope128(x, cos_t, sin_s, lane):
    rot = jnp.where(lane < HALF + C_ROPE // 2, pltpu.roll(x, LANES - C_ROPE // 2, 1),
                    pltpu.roll(x, C_ROPE // 2, 1))
    return x * cos_t + rot * sin_s


def _mla_front_kernel(x_ref, g_ref, wdq_ref, wdkv_ref, gq_ref, gkv_ref, wuq_ref, cos_ref, sin_ref, *rest,
                      fuse_kv):
    if fuse_kv:
        wukv_ref, q_ref, ckvn_ref, krp_ref, krpb_ref, kv_ref = rest
    else:
        q_ref, ckvn_ref, krp_ref, krpb_ref = rest
    tm = x_ref.shape[0]
    lane = lax.broadcasted_iota(jnp.int32, (tm, LANES), 1)
    cos_t = cos_ref[...]
    sin_s = sin_ref[...]
    h = _rms(x_ref[...], g_ref[...]).astype(BF16)
    cq = _rms(jnp.dot(h, wdq_ref[...], preferred_element_type=F32), gq_ref[...]).astype(BF16)
    kva = jnp.dot(h, wdkv_ref[...], preferred_element_type=F32)
    for hd in range(C_HEADS):
        sl = slice(hd * LANES, (hd + 1) * LANES)
        qh = jnp.dot(cq, wuq_ref[:, sl], preferred_element_type=F32)
        q_ref[:, sl] = (_rope128(qh, cos_t, sin_s, lane) * (C_SCALE * LOG2E)).astype(BF16)
    ckv = _rms(kva[:, :C_KV_RANK], gkv_ref[...])
    ckvn_ref[...] = ckv
    krp = _rope128(kva[:, C_KV_RANK:], cos_t, sin_s, lane)
    krp_ref[...] = krp
    krpb_ref[...] = krp.astype(BF16)
    if fuse_kv:
        ckv_b = ckv.astype(BF16)
        for n0 in range(0, kv_ref.shape[1], 512):
            kv_ref[:, n0:n0 + 512] = jnp.dot(ckv_b, wukv_ref[:, n0:n0 + 512],
                                             preferred_element_type=F32).astype(BF16)


def _mla_front(x, gain, w, layer, cos_t, sin_s, t_len, fuse_kv, tm_pref=512):
    m, d = x.shape
    tm = _tile(t_len, tm_pref)
    nt = t_len // tm
    rows = lambda n: pl.BlockSpec((tm, n), lambda i: (i, 0))
    tab = pl.BlockSpec((tm, LANES), lambda i: (i % nt, 0))
    in_specs = [rows(d), _const_spec((1, d)), _const_spec(w["dq"].shape, layer), _const_spec(w["dkv"].shape, layer),
                _const_spec(w["g_q"].shape[1:]), _const_spec(w["g_kv"].shape[1:]), _const_spec(w["uq"].shape, layer),
                tab, tab]
    args = [x, gain.reshape(1, d), w["dq"], w["dkv"], w["g_q"][layer], w["g_kv"][layer], w["uq"], cos_t, sin_s]
    out_specs = [rows(C_HEADS * LANES), rows(C_KV_RANK), rows(LANES), rows(LANES)]
    out_shape = [jax.ShapeDtypeStruct((m, C_HEADS * LANES), BF16), jax.ShapeDtypeStruct((m, C_KV_RANK), F32),
                 jax.ShapeDtypeStruct((m, LANES), F32), jax.ShapeDtypeStruct((m, LANES), BF16)]
    if fuse_kv:
        in_specs.append(_const_spec(w["ukv"].shape, layer))
        args.append(w["ukv"])
        out_specs.append(rows(C_HEADS * LANES))
        out_shape.append(jax.ShapeDtypeStruct((m, C_HEADS * LANES), BF16))
    return pl.pallas_call(
        functools.partial(_mla_front_kernel, fuse_kv=fuse_kv), name="mla_front", grid=(m // tm,),
        in_specs=in_specs, out_specs=out_specs, out_shape=out_shape,
        compiler_params=_cparams("parallel"))(*args)


def _mla_attn_kernel(q_ref, kv_ref, kr_ref, o_ref, *, tq, tk, q_off, n_valid, groups):
    i = pl.program_id(2)
    lane_q = lax.broadcasted_iota(jnp.int32, (tq, LANES), 1)
    qmin = q_off + i * tq
    n_vis = ((qmin + tq - 1) // CHUNK + 1) * CHUNK
    nkb = (jnp.minimum(n_vis, n_valid) + tk - 1) // tk
    n_full = jnp.minimum((qmin | (CHUNK - 1)) + 1, n_valid) // tk
    nh = 2 * groups
    qs = [q_ref[0, :, hd * LANES:(hd + 1) * LANES] for hd in range(nh)]

    def block(row0, nrows, start, nkeys, carry, masked):
        krb = kr_ref[0, pl.ds(start, nkeys), :]
        lane_k = lax.broadcasted_iota(jnp.int32, (nkeys, LANES), 1)
        if masked:
            qpos = qmin + row0 + lax.broadcasted_iota(jnp.int32, (nrows, nkeys), 0)
            klim = jnp.minimum((qpos | (CHUNK - 1)) + 1, n_valid)
            vis = (lax.broadcasted_iota(jnp.int32, (nrows, nkeys), 1) + start) < klim
        new = []
        for hd in range(nh):
            m_run, l_run, acc = carry[3 * hd:3 * hd + 3]
            kvb = kv_ref[0, pl.ds(start, nkeys), hd * LANES:(hd + 1) * LANES]
            kf = jnp.where(lane_k < HALF, kvb, krb)
            s = lax.dot_general(qs[hd][row0:row0 + nrows], kf, (((1,), (1,)), ((), ())),
                                preferred_element_type=F32)
            if masked:
                s = jnp.where(vis, s, NEG_BIG)
            m_new = jnp.maximum(m_run, jnp.max(s, axis=1, keepdims=True))
            alpha = jnp.exp2(m_run - m_new)
            p = jnp.exp2(s - m_new)
            l_new = alpha * l_run + jnp.sum(p, axis=1, keepdims=True)
            acc = alpha * acc + jnp.dot(p.astype(BF16), kvb, preferred_element_type=F32)
            new += [m_new, l_new, acc]
        return tuple(new)

    def full_block(masked):
        def body(kb, carry):
            return block(0, tq, pl.multiple_of(kb * tk, tk), tk, carry, masked)
        return body

    init = (jnp.full((tq, 1), NEG_BIG, F32), jnp.zeros((tq, 1), F32), jnp.zeros((tq, LANES), F32)) * nh
    res = lax.fori_loop(0, n_full, full_block(False), init)
    half = tq // 2
    if tq == tk and q_off % tk == 0 and half % CHUNK == 0 and half % LANES == 0 and n_valid % tk == 0:
        dstart = pl.multiple_of((nkb - 1) * tk, tk)
        top = block(0, half, dstart, half, tuple(a[:half] for a in res), True)
        bot = block(half, half, dstart, tk, tuple(a[half:] for a in res), True)
        res = tuple(jnp.concatenate([a, b], axis=0) for a, b in zip(top, bot))
    else:
        res = lax.fori_loop(n_full, nkb, full_block(True), res)
    for g in range(groups):
        o0 = res[6 * g + 2] / res[6 * g + 1]
        o1 = res[6 * g + 5] / res[6 * g + 4]
        o = jnp.where(lane_q < HALF, pltpu.roll(o0, HALF, 1), o1)
        o_ref[0, :, g * LANES:(g + 1) * LANES] = o.astype(o_ref.dtype)


def _mla_attn(q, kv, krp, q_off, n_valid, tq_pref=512, groups=ATTN_GROUPS):
    bsz, tq_len, _ = q.shape
    tk_len = kv.shape[1]
    tq = _tile(tq_len, tq_pref)
    tk = _key_tile(tk_len)
    assert CHUNK & (CHUNK - 1) == 0 and tk_len % tk == 0 and n_valid <= tk_len and C_HEADS % (2 * groups) == 0
    return pl.pallas_call(
        functools.partial(_mla_attn_kernel, tq=tq, tk=tk, q_off=q_off, n_valid=n_valid, groups=groups),
        name="mla_attn",
        grid=(bsz, C_HEADS // (2 * groups), tq_len // tq),
        in_specs=[pl.BlockSpec((1, tq, 2 * groups * LANES), lambda b, h, i: (b, i, h)),
                  pl.BlockSpec((1, tk_len, 2 * groups * LANES), lambda b, h, i: (b, 0, h)),
                  pl.BlockSpec((1, tk_len, LANES), lambda b, h, i: (b, 0, 0))],
        out_specs=pl.BlockSpec((1, tq, groups * LANES), lambda b, h, i: (b, i, h)),
        out_shape=jax.ShapeDtypeStruct((bsz, tq_len, C_HEADS * C_VDIM), BF16),
        compiler_params=_cparams("parallel", "parallel", "arbitrary"))(q, kv, krp)


def _pad_rows(a, n):
    return jnp.pad(a, ((0, 0), (0, n - a.shape[1]), (0, 0)))


def _mlstm_layer(x, gain, state, conv0, prev, w, j, bsz, t_len):
    xm, z = _proj(x, gain, [(w["up"], j)], [(0, 0, A_INNER, F32), (0, A_INNER, 2 * A_INNER, BF16)],
                  name="mlstm_up")
    xm3 = xm.reshape(bsz, t_len, A_INNER)
    conv0p = jnp.pad(conv0, ((0, 0), (8 - (A_CONV - 1), 0), (0, 0)))
    q, k, v, xc, gcol, grow = _mlstm_front(xm3, conv0p, w["conv_w"][j], w["conv_b"][j], w["wq"], w["wk"],
                                           w["wv"], w["wg"], w["gb"][j], j)
    hn, c_st, n_st, m_st = _mlstm_rec(q, k, v, gcol, grow, w["g_head"][j], state, j, prev)
    mixed = (hn.reshape(bsz * t_len, A_INNER), xc.reshape(bsz * t_len, A_INNER), z, w["skip"][j])
    assert t_len >= A_CONV - 1
    return mixed, (c_st, n_st, m_st), xm3[:, t_len - (A_CONV - 1):]


def _sb_layer(x, gain, past, w, j, bsz, t_len, q_off):
    d = x.shape[1]
    q, k, kb, v, vb = _proj(x, gain, [(w["wqkv"], j)],
                            [(0, 0, d, BF16), (0, d, 2 * d, F32), (0, d, 2 * d, BF16),
                             (0, 2 * d, 3 * d, F32), (0, 2 * d, 3 * d, BF16)], name="sb_qkv")
    k3 = kb.reshape(bsz, t_len, d)
    v3 = vb.reshape(bsz, t_len, d)
    if past is not None:
        k3 = jnp.concatenate([past[0].reshape(bsz, -1, d).astype(BF16), k3], axis=1)
        v3 = jnp.concatenate([past[1].reshape(bsz, -1, d).astype(BF16), v3], axis=1)
    tk_pad = -(-k3.shape[1] // LANES) * LANES
    k3 = _pad_rows(k3, tk_pad)
    v3 = _pad_rows(v3, tk_pad)
    o = _sb_attn(q.reshape(bsz, t_len, d), k3, v3, w["u"], q_off)
    return o.reshape(bsz * t_len, d), (k.reshape(bsz, t_len, B_HEADS, B_DH), v.reshape(bsz, t_len, B_HEADS, B_DH))


def _rope_tables(t_len, q_off):
    half = C_ROPE // 2
    inv = ROPE_THETA ** (-jnp.arange(half, dtype=F32) / half)
    ang = (jnp.arange(t_len, dtype=F32) + q_off)[:, None] * inv[None, :]
    cos, sin = jnp.cos(ang), jnp.sin(ang)
    ones = jnp.ones((t_len, HALF), F32)
    zeros = jnp.zeros((t_len, HALF), F32)
    tail = LANES - HALF - C_ROPE
    cos_t = jnp.concatenate([ones, cos, cos, ones[:, :tail]], axis=1)
    sin_s = jnp.concatenate([zeros, -sin, sin, zeros[:, :tail]], axis=1)
    return cos_t, sin_s


def _mla_layer(x, gain, past, w, j, bsz, t_len, q_off):
    cos_t, sin_s = _rope_tables(t_len, q_off)
    outs = _mla_front(x, gain, w, j, cos_t, sin_s, t_len, fuse_kv=past is None)
    q, ckv, krp, krpb = outs[:4]
    ckv3 = ckv.reshape(bsz, t_len, C_KV_RANK)
    kr_new = krp.reshape(bsz, t_len, LANES)[:, :, HALF:HALF + C_ROPE]
    krpb3 = krpb.reshape(bsz, t_len, LANES)
    if past is None:
        assert t_len % LANES == 0
        n_valid = tk_pad = t_len
        kv, krp_all = outs[4], krpb3
    else:
        ckv_all = jnp.concatenate([past[0], ckv3], axis=1)
        kr_pad = jnp.pad(past[1], ((0, 0), (0, 0), (HALF, LANES - HALF - C_ROPE))).astype(BF16)
        n_valid = ckv_all.shape[1]
        tk_pad = -(-n_valid // LANES) * LANES
        krp_all = _pad_rows(jnp.concatenate([kr_pad, krpb3], axis=1), tk_pad)
        (kv,) = _proj(_pad_rows(ckv_all, tk_pad).reshape(bsz * tk_pad, C_KV_RANK), None, [(w["ukv"], j)],
                      [(0, 0, C_HEADS * LANES, BF16)], name="mla_kv_up")
    o = _mla_attn(q.reshape(bsz, t_len, C_HEADS * LANES), kv.reshape(bsz, tk_pad, C_HEADS * LANES),
                  krp_all, q_off, n_valid)
    return o.reshape(bsz * t_len, C_HEADS * C_VDIM), (ckv3, kr_new)


def _trunk(x3, q_off, st_a, past_b, past_c, wts):
    bsz, t_len, d = x3.shape
    x = x3.reshape(bsz * t_len, d)
    a_states, a_convs, new_b, new_c = None, [], [], []
    state = None
    if st_a is not None:
        m_rep = jnp.broadcast_to(st_a[2][..., None], st_a[2].shape + (LANES,))
        state = (st_a[0], st_a[1], m_rep)
    for i in range(DEPTH):
        j = i // 3
        if i % 3 == 0:
            conv0 = jnp.zeros((bsz, A_CONV - 1, A_INNER), F32) if st_a is None else st_a[3][j]
            mixed, a_states, conv_new = _mlstm_layer(x, wts["norm_mix"][i], state, conv0, a_states,
                                                     wts["a"], j, bsz, t_len)
            a_convs.append(conv_new)
            wo = wts["a"]["down"]
        elif i % 3 == 1:
            past = None if past_b is None else (past_b[0][j], past_b[1][j])
            mixed, kv = _sb_layer(x, wts["norm_mix"][i], past, wts["b"], j, bsz, t_len, q_off)
            new_b.append(kv)
            wo = wts["b"]["wo"]
        else:
            past = None if past_c is None else (past_c[0][j], past_c[1][j])
            mixed, lat = _mla_layer(x, wts["norm_mix"][i], past, wts["c"], j, bsz, t_len, q_off)
            new_c.append(lat)
            wo = wts["c"]["wo"]
        final = wts["norm_final"] if i == DEPTH - 1 else None
        x = _mixer_out_mlp(x, mixed, wo, j, wts["norm_mlp"][i], wts["ff1"], wts["ff2"], i, final_gain=final)
    stack = lambda items, idx: jnp.stack([it[idx] for it in items])
    return (x.reshape(bsz, t_len, d),
            a_states[0], a_states[1], a_states[2][..., 0], jnp.stack(a_convs),
            stack(new_b, 0), stack(new_b, 1), stack(new_c, 0), stack(new_c, 1))


def _prep_weights(norm_mix, norm_mlp, norm_final, a_w_up, a_conv_w, a_conv_b, a_w_q, a_w_k, a_w_v,
                  a_w_gate, a_b_i, a_b_f, a_g_head, a_skip, a_w_down, b_w_qkv, b_w_o, c_w_dq, c_g_q,
                  c_w_uq, c_w_dkv, c_g_kv, c_w_ukv, c_w_o, w_ff1, w_ff2):
    bf = lambda a: a.astype(BF16)
    n_a = a_w_up.shape[0]
    wg = a_w_gate.reshape(n_a, 3, A_HEADS, A_DH, 2 * A_HEADS)
    wg = jnp.pad(wg, ((0, 0),) * 4 + ((0, LANES - 2 * A_HEADS),))
    gb = jnp.pad(jnp.concatenate([a_b_i, a_b_f], axis=1), ((0, 0), (0, LANES - 2 * A_HEADS)))
    a = dict(up=bf(a_w_up), conv_w=a_conv_w, conv_b=a_conv_b.reshape(n_a, 1, A_INNER), wq=bf(a_w_q),
             wk=bf(a_w_k), wv=bf(a_w_v), wg=bf(wg), gb=gb.reshape(n_a, 1, LANES),
             g_head=a_g_head.reshape(n_a, 1, A_INNER), skip=a_skip.reshape(n_a, 1, A_INNER), down=bf(a_w_down))
    d = b_w_o.shape[1]
    qscale = jnp.where(jnp.arange(3 * d) < d, (B_DH ** -0.5) * LOG2E, 1.0).astype(F32)
    jj = lax.broadcasted_iota(jnp.int32, (2 * LANES, 2 * LANES), 0) % LANES
    ss = lax.broadcasted_iota(jnp.int32, (2 * LANES, 2 * LANES), 1)
    b = dict(wqkv=bf(b_w_qkv * qscale), wo=bf(b_w_o), u=-((ss >= LANES) | (jj > ss)).astype(BF16))
    n_c = c_w_dq.shape[0]
    uq = c_w_uq.reshape(n_c, -1, C_HEADS, C_NOPE + C_ROPE)
    uq = jnp.pad(uq, ((0, 0),) * 3 + ((0, LANES - C_NOPE - C_ROPE),)).reshape(n_c, -1, C_HEADS * LANES)
    dkv = jnp.concatenate([c_w_dkv[..., :C_KV_RANK], jnp.zeros(c_w_dkv.shape[:2] + (HALF,), F32),
                           c_w_dkv[..., C_KV_RANK:],
                           jnp.zeros(c_w_dkv.shape[:2] + (LANES - HALF - C_ROPE,), F32)], axis=-1)
    c = dict(dq=bf(c_w_dq), dkv=bf(dkv), g_q=c_g_q.reshape(n_c, 1, -1), g_kv=c_g_kv.reshape(n_c, 1, -1),
             uq=bf(uq), ukv=bf(c_w_ukv.reshape(n_c, C_KV_RANK, C_HEADS * LANES)), wo=bf(c_w_o))
    return dict(norm_mix=norm_mix, norm_mlp=norm_mlp, norm_final=norm_final, a=a, b=b, c=c,
                ff1=bf(w_ff1), ff2=bf(w_ff2))


def kernel(x_prompt, x_sample, state_mlstm_C, state_mlstm_n, state_mlstm_m, state_mlstm_conv, cache_sb_k, cache_sb_v, cache_mla_ckv, cache_mla_krope, norm_mix, norm_mlp, norm_final, a_w_up, a_conv_w, a_conv_b, a_w_q, a_w_k, a_w_v, a_w_gate, a_b_i, a_b_f, a_g_head, a_skip, a_w_down, b_w_qkv, b_w_o, c_w_dq, c_g_q, c_w_uq, c_w_dkv, c_g_kv, c_w_ukv, c_w_o, w_ff1, w_ff2):
    wts = _prep_weights(norm_mix, norm_mlp, norm_final, a_w_up, a_conv_w, a_conv_b, a_w_q, a_w_k, a_w_v,
                        a_w_gate, a_b_i, a_b_f, a_g_head, a_skip, a_w_down, b_w_qkv, b_w_o, c_w_dq, c_g_q,
                        c_w_uq, c_w_dkv, c_g_kv, c_w_ukv, c_w_o, w_ff1, w_ff2)
    outs_p = _trunk(x_prompt, 0, None, None, None, wts)
    past_len = cache_sb_k.shape[2]
    outs_s = _trunk(x_sample, past_len,
                    (state_mlstm_C, state_mlstm_n, state_mlstm_m, state_mlstm_conv),
                    (cache_sb_k, cache_sb_v), (cache_mla_ckv, cache_mla_krope), wts)
    y_p, rest_p = outs_p[0], outs_p[1:]
    y_s, rest_s = outs_s[0], outs_s[1:]
    return (y_p, y_s) + tuple(rest_p) + tuple(rest_s)
```

```python
import functools

import jax
import jax.numpy as jnp
from jax import lax
from jax.experimental import pallas as pl
from jax.experimental.pallas import tpu as pltpu

F32 = jnp.float32
BF16 = jnp.bfloat16

EPS = 1e-6
DEPTH = 4
CHUNK = 64
A_HEADS = 4
A_DH = 512
A_INNER = A_HEADS * A_DH
A_CONV = 4
B_HEADS = 16
B_DH = 64
C_HEADS = 16
C_NOPE = 64
C_ROPE = 32
C_VDIM = 64
C_KV_RANK = 256
C_SCALE = (C_NOPE + C_ROPE) ** -0.5
ROPE_THETA = 10000.0

LANES = 128
MXU_DIM = 256
HALF = LANES // 2
NEG_BIG = -1e30
LOG2E = 1.4426950408889634
VMEM_LIMIT_BYTES = 56 * 1024 * 1024


def _cparams(*sem):
    return pltpu.CompilerParams(dimension_semantics=sem, vmem_limit_bytes=VMEM_LIMIT_BYTES)


def _tile(n, pref):
    if n <= pref:
        return n
    t = pref
    while n % t:
        t //= 2
    return t


def _const_spec(shape, layer=None):
    nd = len(shape)
    if layer is None:
        return pl.BlockSpec(shape, lambda *_: (0,) * nd)
    return pl.BlockSpec((None,) + tuple(shape[1:]), lambda *_: (layer,) + (0,) * (nd - 1))


def _rms(x, g):
    return x * lax.rsqrt(jnp.mean(x * x, axis=-1, keepdims=True) + EPS) * g


def _log_sigmoid(x):
    return jnp.minimum(x, 0.0) - jnp.log1p(jnp.exp(-jnp.abs(x)))


def _proj_kernel(*refs, n_w, outs, has_norm, has_res):
    refs = list(refs)
    x_ref = refs.pop(0)
    g_ref = refs.pop(0) if has_norm else None
    w_refs = [refs.pop(0) for _ in range(n_w)]
    r_ref = refs.pop(0) if has_res else None
    if has_norm:
        h = _rms(x_ref[...].astype(F32), g_ref[...]).astype(BF16)
    else:
        h = x_ref[...].astype(BF16)
    groups = {}
    for o_ref, (wi, c0, c1, _) in zip(refs, outs):
        groups.setdefault((wi, c0, c1), []).append(o_ref)
    for gi, ((wi, c0, c1), dests) in enumerate(groups.items()):
        n = c1 - c0
        c = n if n <= 512 else 512
        for n0 in range(0, n, c):
            y = jnp.dot(h, w_refs[wi][:, c0 + n0:c0 + n0 + c], preferred_element_type=F32)
            if has_res and gi == 0:
                y = y + r_ref[:, n0:n0 + c]
            for o_ref in dests:
                o_ref[:, n0:n0 + c] = y.astype(o_ref.dtype)


def _proj(x, gain, weights, outs, residual=None, tm_pref=512, name="proj"):
    m, k = x.shape
    tm = _tile(m, tm_pref)
    in_specs = [pl.BlockSpec((tm, k), lambda i: (i, 0))]
    args = [x]
    if gain is not None:
        in_specs.append(_const_spec((1, k)))
        args.append(gain.reshape(1, k).astype(F32))
    for w, layer in weights:
        in_specs.append(_const_spec(w.shape, layer))
        args.append(w)
    if residual is not None:
        in_specs.append(pl.BlockSpec((tm, residual.shape[1]), lambda i: (i, 0)))
        args.append(residual)
    out_shape = [jax.ShapeDtypeStruct((m, c1 - c0), dt) for _, c0, c1, dt in outs]
    out_specs = [pl.BlockSpec((tm, c1 - c0), lambda i: (i, 0)) for _, c0, c1, _ in outs]
    kern = functools.partial(_proj_kernel, n_w=len(weights), outs=tuple(outs),
                             has_norm=gain is not None, has_res=residual is not None)
    return pl.pallas_call(kern, name=name, grid=(m // tm,), in_specs=in_specs, out_specs=out_specs,
                          out_shape=out_shape, compiler_params=_cparams("parallel"))(*args)


def _mlp_kernel(*refs, pre, final, tf):
    refs = list(refs)
    x_ref = refs.pop(0)
    if pre == "mlstm":
        hn_ref, xc_ref, z_ref, skip_ref = (refs.pop(0) for _ in range(4))
        z = z_ref[...].astype(F32)
        a = (hn_ref[...].astype(F32) + skip_ref[...] * xc_ref[...].astype(F32)) * (1.0 / (1.0 + jnp.exp(-z)))
        a = a.astype(BF16)
    else:
        a = refs.pop(0)[...]
    wo_ref, g_ref, w1_ref, w2_ref = (refs.pop(0) for _ in range(4))
    gf_ref = refs.pop(0) if final else None
    (o_ref,) = refs
    x = x_ref[...] + jnp.dot(a, wo_ref[...], preferred_element_type=F32)
    h = _rms(x, g_ref[...]).astype(BF16)
    acc = x
    for f0 in range(0, w1_ref.shape[1], tf):
        u = jnp.dot(h, w1_ref[:, f0:f0 + tf], preferred_element_type=F32)
        u = jnp.maximum(u, 0.0)
        u = u * u
        acc = acc + jnp.dot(u.astype(BF16), w2_ref[f0:f0 + tf, :], preferred_element_type=F32)
    if final:
        acc = _rms(acc, gf_ref[...])
    o_ref[...] = acc


def _mixer_out_mlp(x, mixed, wo, wo_layer, gain, w1, w2, layer, final_gain=None, tm_pref=512, tf_pref=1024):
    m, d = x.shape
    dff = w1.shape[2]
    tm = _tile(m, tm_pref)
    tf = _tile(dff, tf_pref)
    rows = lambda n: pl.BlockSpec((tm, n), lambda i: (i, 0))
    resident = lambda arr, lyr: pl.BlockSpec((None,) + arr.shape[1:], lambda i: (lyr, 0, 0),
                                             pipeline_mode=pl.Buffered(1))
    in_specs, args = [rows(d)], [x]
    if isinstance(mixed, tuple):
        hn, xc, z, skip = mixed
        pre = "mlstm"
        in_specs += [rows(hn.shape[1]), rows(xc.shape[1]), rows(z.shape[1]), _const_spec(skip.shape)]
        args += [hn, xc, z, skip]
    else:
        pre = "proj"
        in_specs.append(rows(mixed.shape[1]))
        args.append(mixed)
    in_specs += [resident(wo, wo_layer), _const_spec((1, d)), resident(w1, layer), resident(w2, layer)]
    args += [wo, gain.reshape(1, d), w1, w2]
    if final_gain is not None:
        in_specs.append(_const_spec((1, d)))
        args.append(final_gain.reshape(1, d))
    return pl.pallas_call(
        functools.partial(_mlp_kernel, pre=pre, final=final_gain is not None, tf=tf), name="out_mlp",
        grid=(m // tm,), in_specs=in_specs, out_specs=rows(d),
        out_shape=jax.ShapeDtypeStruct((m, d), F32),
        compiler_params=_cparams("parallel"))(*args)


HALO = 16


def _mlstm_front_kernel(x_ref, xh_ref, conv0_ref, gn_ref, wup_ref, cw_ref, cb_ref, wq_ref, wk_ref, wv_ref,
                        wg_ref, gb_ref, q_ref, k_ref, v_ref, xc_ref, z_ref, gcol_ref, grow_ref, tail_ref,
                        xp_ref, *, tt):
    t = pl.program_id(1)
    hn = _rms(x_ref[0], gn_ref[...]).astype(BF16)
    hh = _rms(xh_ref[0], gn_ref[...]).astype(BF16)
    xms = []
    for h in range(A_HEADS):
        sl = slice(h * A_DH, (h + 1) * A_DH)
        xm = jnp.dot(hn, wup_ref[:, sl], preferred_element_type=F32)
        xm_prev = jnp.dot(hh, wup_ref[:, sl], preferred_element_type=F32)[HALO - 8:]
        xp_ref[0:8, sl] = jnp.where(t == 0, conv0_ref[0, :, sl], xm_prev)
        xp_ref[8:8 + tt, sl] = xm
        xms.append(xm)
    for h in range(A_HEADS):
        z_ref[0, :, h * A_DH:(h + 1) * A_DH] = jnp.dot(
            hn, wup_ref[:, A_INNER + h * A_DH:A_INNER + (h + 1) * A_DH], preferred_element_type=F32).astype(BF16)
    g = jnp.zeros((tt, LANES), F32)
    for h in range(A_HEADS):
        sl = slice(h * A_DH, (h + 1) * A_DH)
        xm = xms[h]
        xc = cb_ref[:, sl] + xm * cw_ref[A_CONV - 1:A_CONV, sl]
        for j in range(A_CONV - 1):
            xc = xc + xp_ref[5 + j:5 + j + tt, sl] * cw_ref[j:j + 1, sl]
        xc = xc / (1.0 + jnp.exp(-xc))
        xch = xc.astype(BF16)
        xc_ref[0, :, sl] = xch
        qh = jnp.dot(xch, wq_ref[h], preferred_element_type=F32).astype(BF16)
        kh = (jnp.dot(xch, wk_ref[h], preferred_element_type=F32) * (A_DH ** -0.5)).astype(BF16)
        vh = jnp.dot(xm.astype(BF16), wv_ref[h], preferred_element_type=F32).astype(BF16)
        q_ref[0, :, sl] = qh
        k_ref[0, :, sl] = kh
        v_ref[0, :, sl] = vh
        g = g + jnp.dot(qh, wg_ref[0, h], preferred_element_type=F32)
        g = g + jnp.dot(kh, wg_ref[1, h], preferred_element_type=F32)
        g = g + jnp.dot(vh, wg_ref[2, h], preferred_element_type=F32)
    tail_ref[0] = xp_ref[tt:tt + 8, :]
    g = g + gb_ref[...]
    lane = lax.broadcasted_iota(jnp.int32, (tt, LANES), 1)
    g = jnp.where(lane < A_HEADS, g, _log_sigmoid(g))
    gcol_ref[0] = g
    sel = (lax.broadcasted_iota(jnp.int32, (8, LANES), 0) ==
           lax.broadcasted_iota(jnp.int32, (8, LANES), 1)).astype(BF16)
    grow = jnp.zeros((8, tt), F32)
    rem = g
    for _ in range(3):
        part = rem.astype(BF16)
        grow = grow + lax.dot_general(sel, part, (((1,), (1,)), ((), ())), preferred_element_type=F32)
        rem = rem - part.astype(F32)
    grow_ref[0] = grow


def _mlstm_front(x3, gain, conv0p, w, layer, tt_pref=256):
    bsz, t_len, d = x3.shape
    tt = _tile(t_len, tt_pref)
    nhalo = tt // HALO
    resident = lambda arr: pl.BlockSpec((None,) + arr.shape[1:], lambda b, t, nd=arr.ndim: (layer,) + (0,) * (nd - 1),
                                        pipeline_mode=pl.Buffered(1))
    act = lambda dt: jax.ShapeDtypeStruct((bsz, t_len, A_INNER), dt)
    act_spec = pl.BlockSpec((1, tt, A_INNER), lambda b, t: (b, t, 0))
    cw, cb, gb = w["conv_w"][layer], w["conv_b"][layer], w["gb"][layer]
    return pl.pallas_call(
        functools.partial(_mlstm_front_kernel, tt=tt), name="mlstm_front",
        grid=(bsz, t_len // tt),
        in_specs=[pl.BlockSpec((1, tt, d), lambda b, t: (b, t, 0)),
                  pl.BlockSpec((1, HALO, d), lambda b, t: (b, jnp.maximum(t * nhalo - 1, 0), 0)),
                  pl.BlockSpec((1, 8, A_INNER), lambda b, t: (b, 0, 0)),
                  _const_spec((1, d)), resident(w["up"]),
                  _const_spec(cw.shape), _const_spec(cb.shape), resident(w["wq"]), resident(w["wk"]),
                  resident(w["wv"]), resident(w["wg"]), _const_spec(gb.shape)],
        out_specs=[act_spec, act_spec, act_spec, act_spec, act_spec,
                   pl.BlockSpec((1, tt, LANES), lambda b, t: (b, t, 0)),
                   pl.BlockSpec((1, 8, tt), lambda b, t: (b, 0, t)),
                   pl.BlockSpec((1, 8, A_INNER), lambda b, t: (b, 0, 0))],
        out_shape=[act(BF16), act(BF16), act(BF16), act(BF16), act(BF16),
                   jax.ShapeDtypeStruct((bsz, t_len, LANES), F32),
                   jax.ShapeDtypeStruct((bsz, 8, t_len), F32),
                   jax.ShapeDtypeStruct((bsz, 8, A_INNER), F32)],
        scratch_shapes=[pltpu.VMEM((tt + 8, A_INNER), F32)],
        compiler_params=_cparams("parallel", "arbitrary"))(
            x3, x3, conv0p, gain.reshape(1, d), w["up"], cw, cb, w["wq"], w["wk"], w["wv"], w["wg"], gb)


def _mlstm_rec_kernel(*refs, lc, has_state, n_prev):
    refs = list(refs)
    q_ref, k_ref, v_ref, gcol_ref, grow_ref, gh_ref = (refs.pop(0) for _ in range(6))
    init_refs = [refs.pop(0) for _ in range(3)] if has_state else None
    prev_refs = [refs.pop(0) for _ in range(3)] if n_prev else None
    hn_ref, cs_ref, ns_ref, ms_ref = refs
    c_ref, n_ref, m_ref = cs_ref.at[n_prev], ns_ref.at[n_prev], ms_ref.at[n_prev]
    c_idx = pl.program_id(1)

    @pl.when(c_idx == 0)
    def _():
        for st_ref, i_ref in zip((c_ref, n_ref, m_ref), init_refs or (None,) * 3):
            st_ref[...] = jnp.zeros_like(st_ref) if i_ref is None else i_ref[...]
        if n_prev:
            for out_ref, p_ref in zip((cs_ref, ns_ref, ms_ref), prev_refs):
                out_ref[0:n_prev] = p_ref[...]

    row = lax.broadcasted_iota(jnp.int32, (lc, lc), 0)
    col = lax.broadcasted_iota(jnp.int32, (lc, lc), 1)
    causal = col <= row
    gcol = gcol_ref[0]
    grow = grow_ref[0]
    for h in range(A_HEADS):
        sl = slice(h * A_DH, (h + 1) * A_DH)
        qh = q_ref[0, :, sl]
        kh = k_ref[0, :, sl]
        vh = v_ref[0, :, sl]
        li_c = gcol[:, h:h + 1]
        lf_c = gcol[:, A_HEADS + h:A_HEADS + h + 1]
        li_r = grow[h:h + 1, :]
        lf_r = grow[A_HEADS + h:A_HEADS + h + 1, :]
        b_c = jnp.sum(jnp.where(causal, lf_r, 0.0), axis=1, keepdims=True)
        b_r = jnp.sum(jnp.where(row <= col, lf_c, 0.0), axis=0, keepdims=True)
        c_old = c_ref[0, h]
        n_old = n_ref[0, h:h + 1, :]
        m_old = m_ref[0, h:h + 1, 0:1]
        dmat = jnp.where(causal, b_c - b_r + li_r, NEG_BIG)
        inter = b_c + m_old
        m_t = jnp.maximum(inter, jnp.max(dmat, axis=1, keepdims=True))
        w_inter = jnp.exp(inter - m_t)
        s = lax.dot_general(qh, kh, (((1,), (1,)), ((), ())), preferred_element_type=F32)
        s = s * jnp.exp(dmat - m_t)
        num = w_inter * jnp.dot(qh, c_old.astype(BF16), preferred_element_type=F32)
        num = num + jnp.dot(s.astype(BF16), vh, preferred_element_type=F32)
        nq = w_inter * jnp.sum(qh.astype(F32) * n_old, axis=1, keepdims=True)
        nq = nq + jnp.sum(s, axis=1, keepdims=True)
        hv = num / jnp.maximum(jnp.abs(nq), jnp.exp(-m_t))
        mu = jnp.mean(hv, axis=1, keepdims=True)
        hc = hv - mu
        var = jnp.mean(hc * hc, axis=1, keepdims=True)
        hn_ref[0, :, sl] = (hc * lax.rsqrt(var + EPS) * gh_ref[:, sl]).astype(hn_ref.dtype)
        m_new = m_t[lc - 1:lc, :]
        b_last = b_c[lc - 1:lc, :]
        g_state = jnp.exp(b_last + m_old - m_new)
        g_tok = jnp.exp(b_last - b_c + li_c - m_new)
        kw = kh.astype(F32) * g_tok
        c_ref[0, h] = g_state * c_old + jnp.dot(kw.T.astype(BF16), vh, preferred_element_type=F32)
        n_ref[0, h:h + 1, :] = g_state * n_old + jnp.sum(kw, axis=0, keepdims=True)
        m_ref[0, h:h + 1, :] = jnp.broadcast_to(m_new, (1, LANES))


def _mlstm_rec(q, k, v, gcol, grow, g_head, state, layer, prev, lc_pref=256):
    bsz, t_len, _ = q.shape
    lc = _tile(t_len, lc_pref)
    n_prev = 0 if prev is None else prev[0].shape[0]
    act_spec = pl.BlockSpec((1, lc, A_INNER), lambda b, c: (b, c, 0))
    st_shapes = [(1, A_HEADS, A_DH, A_DH), (1, A_HEADS, A_DH), (1, A_HEADS, LANES)]
    stacked = lambda n: [pl.BlockSpec((n,) + shp, lambda b, c, nd=len(shp): (0, b) + (0,) * (nd - 1))
                         for shp in st_shapes]
    in_specs = [act_spec, act_spec, act_spec,
                pl.BlockSpec((1, lc, LANES), lambda b, c: (b, c, 0)),
                pl.BlockSpec((1, 8, lc), lambda b, c: (b, 0, c)),
                _const_spec((1, A_INNER))]
    args = [q, k, v, gcol, grow, g_head]
    if state is not None:
        in_specs += [pl.BlockSpec((None,) + shp, lambda b, c, nd=len(shp): (layer, b) + (0,) * (nd - 1))
                     for shp in st_shapes]
        args += list(state)
    if n_prev:
        in_specs += stacked(n_prev)
        args += list(prev)
    return pl.pallas_call(
        functools.partial(_mlstm_rec_kernel, lc=lc, has_state=state is not None, n_prev=n_prev),
        name="mlstm_rec", grid=(bsz, t_len // lc), in_specs=in_specs,
        out_specs=[act_spec] + stacked(n_prev + 1),
        out_shape=[jax.ShapeDtypeStruct((bsz, t_len, A_INNER), BF16)] +
                  [jax.ShapeDtypeStruct((n_prev + 1, bsz) + shp[1:], F32) for shp in st_shapes],
        compiler_params=_cparams("parallel", "arbitrary"))(*args)


def _sb_attn_kernel(q_ref, k_ref, v_ref, u_ref, o_ref, *, tq, tk, q_off, groups):
    lane = lax.broadcasted_iota(jnp.int32, (tq, LANES), 1)
    u = u_ref[...]
    n_carry = 4 * groups
    half = tq // 2

    def block(qms, qmin, row0, nrows, start, nkeys, carry, masked):
        nsub = nkeys // LANES
        if masked:
            qpos = qmin + row0 + lax.broadcasted_iota(jnp.int32, (nrows, nkeys), 0)
            before = (lax.broadcasted_iota(jnp.int32, (nrows, nkeys), 1) + start) < qpos
        new = []
        for hd in range(2 * groups):
            sl = slice((hd // 2) * LANES, (hd // 2 + 1) * LANES)
            k2 = k_ref[0, start:start + nkeys, sl]
            v2 = v_ref[0, start:start + nkeys, sl]
            acc, run = carry[2 * hd], carry[2 * hd + 1]
            z = lax.dot_general(qms[hd][row0:row0 + nrows], k2, (((1,), (1,)), ((), ())),
                                preferred_element_type=F32)
            sp = jnp.maximum(z, 0.0) + jnp.log(1.0 + jnp.exp2(-jnp.abs(z))) * LOG2E
            nlf = jnp.where(before, sp, 0.0) if masked else sp
            hi = nlf.astype(BF16)
            lo = (nlf - hi.astype(F32)).astype(BF16)
            pieces = [jnp.concatenate([hi[:, sb * LANES:(sb + 1) * LANES], lo[:, sb * LANES:(sb + 1) * LANES]], axis=1)
                      for sb in range(nsub)]
            if nsub > 1 and nrows < MXU_DIM:
                r_all = jnp.dot(jnp.concatenate(pieces, axis=0), u, preferred_element_type=F32)
                rs = [r_all[sb * nrows:(sb + 1) * nrows] for sb in range(nsub)]
            else:
                rs = [jnp.dot(p, u, preferred_element_type=F32) for p in pieces]
            rests = [None] * nsub
            for sb in reversed(range(nsub)):
                rests[sb] = rs[sb][:, :LANES] + run
                run = run + rs[sb][:, LANES:]
            rest = rests[0] if nsub == 1 else jnp.concatenate(rests, axis=1)
            att = jnp.exp2(z - sp + rest)
            if masked:
                att = jnp.where(before, att, 0.0)
            acc = acc + jnp.dot(att.astype(BF16), v2, preferred_element_type=F32)
            new += [acc, run]
        return tuple(new)

    for i in range(q_ref.shape[1] // tq):
        rows = slice(i * tq, (i + 1) * tq)
        qmin = q_off + i * tq
        nkb = (qmin + tq - 2) // tk + 1
        n_diag = nkb - qmin // tk
        qms = []
        for g in range(groups):
            q2 = q_ref[0, rows, g * LANES:(g + 1) * LANES]
            qms += [jnp.where((lane >= HALF * e) & (lane < HALF * (e + 1)), q2, jnp.zeros_like(q2))
                    for e in range(2)]
        if tq == tk and qmin % tk == 0 and half % LANES == 0:
            zeros = jnp.zeros((half, LANES), F32)
            top = block(qms, qmin, 0, half, qmin, half, (zeros,) * n_carry, True)
            bot = block(qms, qmin, half, half, qmin, tk, (zeros,) * n_carry, True)
            res = tuple(jnp.concatenate([a, b], axis=0) for a, b in zip(top, bot))
            n_diag = 1
        else:
            res = (jnp.zeros((tq, LANES), F32),) * n_carry
            for j in range(n_diag):
                res = block(qms, qmin, 0, tq, (nkb - 1 - j) * tk, tk, res, True)
        for j in range(n_diag, nkb):
            res = block(qms, qmin, 0, tq, (nkb - 1 - j) * tk, tk, res, False)
        for g in range(groups):
            o_ref[0, rows, g * LANES:(g + 1) * LANES] = jnp.where(
                lane < HALF, res[4 * g], res[4 * g + 2]).astype(o_ref.dtype)


def _key_tile(tk_len):
    return tk_len if tk_len <= 10 * LANES else 4 * LANES


def _attn_groups(tq_len):
    return 4 if tq_len < LANES else 1


def _sb_attn(q, k_all, v_all, u, q_off, tq_pref=512):
    bsz, tq_len, d = q.shape
    groups = _attn_groups(tq_len)
    tk_len = k_all.shape[1]
    tq = _tile(tq_len, tq_pref)
    tk = _key_tile(tk_len)
    gw = groups * LANES
    assert tk_len % tk == 0 and (q_off + tq_len - 2) // tk < tk_len // tk and d % gw == 0
    return pl.pallas_call(
        functools.partial(_sb_attn_kernel, tq=tq, tk=tk, q_off=q_off, groups=groups), name="sb_attn",
        grid=(bsz, d // gw),
        in_specs=[pl.BlockSpec((1, tq_len, gw), lambda b, h: (b, 0, h)),
                  pl.BlockSpec((1, tk_len, gw), lambda b, h: (b, 0, h)),
                  pl.BlockSpec((1, tk_len, gw), lambda b, h: (b, 0, h)),
                  _const_spec(u.shape)],
        out_specs=pl.BlockSpec((1, tq_len, gw), lambda b, h: (b, 0, h)),
        out_shape=jax.ShapeDtypeStruct((bsz, tq_len, d), BF16),
        compiler_params=_cparams("parallel", "parallel"))(q, k_all, v_all, u)


def _rope128(x, cos_t, sin_s, lane):
    rot = jnp.where(lane < HALF + C_ROPE // 2, pltpu.roll(x, LANES - C_ROPE // 2, 1),
                    pltpu.roll(x, C_ROPE // 2, 1))
    return x * cos_t + rot * sin_s


def _mla_front_kernel(x_ref, g_ref, wdq_ref, wdkv_ref, gq_ref, gkv_ref, wuq_ref, cos_ref, sin_ref, *rest,
                      fuse_kv):
    if fuse_kv:
        wukv_ref, q_ref, ckvn_ref, krp_ref, krpb_ref, kv_ref = rest
    else:
        q_ref, ckvn_ref, krp_ref, krpb_ref = rest
    tm = x_ref.shape[0]
    lane = lax.broadcasted_iota(jnp.int32, (tm, LANES), 1)
    cos_t = cos_ref[...]
    sin_s = sin_ref[...]
    h = _rms(x_ref[...], g_ref[...]).astype(BF16)
    cq = _rms(jnp.dot(h, wdq_ref[...], preferred_element_type=F32), gq_ref[...]).astype(BF16)
    kva = jnp.dot(h, wdkv_ref[...], preferred_element_type=F32)
    for hd in range(C_HEADS):
        sl = slice(hd * LANES, (hd + 1) * LANES)
        qh = jnp.dot(cq, wuq_ref[:, sl], preferred_element_type=F32)
        q_ref[:, sl] = (_rope128(qh, cos_t, sin_s, lane) * (C_SCALE * LOG2E)).astype(BF16)
    ckv = _rms(kva[:, :C_KV_RANK], gkv_ref[...])
    ckvn_ref[...] = ckv
    krp = _rope128(kva[:, C_KV_RANK:], cos_t, sin_s, lane)
    krp_ref[...] = krp
    krpb_ref[...] = krp.astype(BF16)
    if fuse_kv:
        ckv_b = ckv.astype(BF16)
        for n0 in range(0, kv_ref.shape[1], 512):
            kv_ref[:, n0:n0 + 512] = jnp.dot(ckv_b, wukv_ref[:, n0:n0 + 512],
                                             preferred_element_type=F32).astype(BF16)


def _mla_front(x, gain, w, layer, cos_t, sin_s, t_len, fuse_kv, tm_pref=512):
    m, d = x.shape
    tm = _tile(t_len, tm_pref)
    nt = t_len // tm
    rows = lambda n: pl.BlockSpec((tm, n), lambda i: (i, 0))
    tab = pl.BlockSpec((tm, LANES), lambda i: (i % nt, 0))
    in_specs = [rows(d), _const_spec((1, d)), _const_spec(w["dq"].shape, layer), _const_spec(w["dkv"].shape, layer),
                _const_spec(w["g_q"].shape[1:]), _const_spec(w["g_kv"].shape[1:]), _const_spec(w["uq"].shape, layer),
                tab, tab]
    args = [x, gain.reshape(1, d), w["dq"], w["dkv"], w["g_q"][layer], w["g_kv"][layer], w["uq"], cos_t, sin_s]
    out_specs = [rows(C_HEADS * LANES), rows(C_KV_RANK), rows(LANES), rows(LANES)]
    out_shape = [jax.ShapeDtypeStruct((m, C_HEADS * LANES), BF16), jax.ShapeDtypeStruct((m, C_KV_RANK), F32),
                 jax.ShapeDtypeStruct((m, LANES), F32), jax.ShapeDtypeStruct((m, LANES), BF16)]
    if fuse_kv:
        in_specs.append(_const_spec(w["ukv"].shape, layer))
        args.append(w["ukv"])
        out_specs.append(rows(C_HEADS * LANES))
        out_shape.append(jax.ShapeDtypeStruct((m, C_HEADS * LANES), BF16))
    return pl.pallas_call(
        functools.partial(_mla_front_kernel, fuse_kv=fuse_kv), name="mla_front", grid=(m // tm,),
        in_specs=in_specs, out_specs=out_specs, out_shape=out_shape,
        compiler_params=_cparams("parallel"))(*args)


def _mla_attn_kernel(q_ref, kv_ref, kr_ref, o_ref, *, tq, tk, q_off, n_valid, groups):
    lane_q = lax.broadcasted_iota(jnp.int32, (tq, LANES), 1)
    nh = 2 * groups
    half = tq // 2

    def block(qs, qmin, row0, nrows, start, nkeys, carry, masked):
        krb = kr_ref[0, start:start + nkeys, :]
        lane_k = lax.broadcasted_iota(jnp.int32, (nkeys, LANES), 1)
        if masked:
            qpos = qmin + row0 + lax.broadcasted_iota(jnp.int32, (nrows, nkeys), 0)
            klim = jnp.minimum((qpos | (CHUNK - 1)) + 1, n_valid)
            vis = (lax.broadcasted_iota(jnp.int32, (nrows, nkeys), 1) + start) < klim
        new = []
        for hd in range(nh):
            m_run, l_run, acc = carry[3 * hd:3 * hd + 3]
            kvb = kv_ref[0, start:start + nkeys, hd * LANES:(hd + 1) * LANES]
            kf = jnp.where(lane_k < HALF, kvb, krb)
            s = lax.dot_general(qs[hd][row0:row0 + nrows], kf, (((1,), (1,)), ((), ())),
                                preferred_element_type=F32)
            if masked:
                s = jnp.where(vis, s, NEG_BIG)
            m_new = jnp.maximum(m_run, jnp.max(s, axis=1, keepdims=True))
            alpha = jnp.exp2(m_run - m_new)
            p = jnp.exp2(s - m_new)
            l_new = alpha * l_run + jnp.sum(p, axis=1, keepdims=True)
            acc = alpha * acc + jnp.dot(p.astype(BF16), kvb, preferred_element_type=F32)
            new += [m_new, l_new, acc]
        return tuple(new)

    for i in range(q_ref.shape[1] // tq):
        rows = slice(i * tq, (i + 1) * tq)
        qmin = q_off + i * tq
        n_vis = ((qmin + tq - 1) // CHUNK + 1) * CHUNK
        nkb = (min(n_vis, n_valid) + tk - 1) // tk
        n_full = min((qmin | (CHUNK - 1)) + 1, n_valid) // tk
        qs = [q_ref[0, rows, hd * LANES:(hd + 1) * LANES] for hd in range(nh)]
        res = (jnp.full((tq, 1), NEG_BIG, F32), jnp.zeros((tq, 1), F32), jnp.zeros((tq, LANES), F32)) * nh
        for kb in range(n_full):
            res = block(qs, qmin, 0, tq, kb * tk, tk, res, False)
        if (tq == tk and qmin % tk == 0 and half % CHUNK == 0 and half % LANES == 0 and n_valid % tk == 0
                and nkb == n_full + 1):
            top = block(qs, qmin, 0, half, qmin, half, tuple(a[:half] for a in res), True)
            bot = block(qs, qmin, half, half, qmin, tk, tuple(a[half:] for a in res), True)
            res = tuple(jnp.concatenate([a, b], axis=0) for a, b in zip(top, bot))
        else:
            for kb in range(n_full, nkb):
                res = block(qs, qmin, 0, tq, kb * tk, tk, res, True)
        for g in range(groups):
            o0 = res[6 * g + 2] / res[6 * g + 1]
            o1 = res[6 * g + 5] / res[6 * g + 4]
            o = jnp.where(lane_q < HALF, pltpu.roll(o0, HALF, 1), o1)
            o_ref[0, rows, g * LANES:(g + 1) * LANES] = o.astype(o_ref.dtype)


def _mla_attn(q, kv, krp, q_off, n_valid, tq_pref=512):
    bsz, tq_len, _ = q.shape
    groups = _attn_groups(tq_len)
    tk_len = kv.shape[1]
    tq = _tile(tq_len, tq_pref)
    tk = _key_tile(tk_len)
    assert CHUNK & (CHUNK - 1) == 0 and tk_len % tk == 0 and n_valid <= tk_len and C_HEADS % (2 * groups) == 0
    return pl.pallas_call(
        functools.partial(_mla_attn_kernel, tq=tq, tk=tk, q_off=q_off, n_valid=n_valid, groups=groups),
        name="mla_attn",
        grid=(bsz, C_HEADS // (2 * groups)),
        in_specs=[pl.BlockSpec((1, tq_len, 2 * groups * LANES), lambda b, h: (b, 0, h)),
                  pl.BlockSpec((1, tk_len, 2 * groups * LANES), lambda b, h: (b, 0, h)),
                  pl.BlockSpec((1, tk_len, LANES), lambda b, h: (b, 0, 0))],
        out_specs=pl.BlockSpec((1, tq_len, groups * LANES), lambda b, h: (b, 0, h)),
        out_shape=jax.ShapeDtypeStruct((bsz, tq_len, C_HEADS * C_VDIM), BF16),
        compiler_params=_cparams("parallel", "parallel"))(q, kv, krp)


def _pad_rows(a, n):
    return jnp.pad(a, ((0, 0), (0, n - a.shape[1]), (0, 0)))


def _mlstm_layer(x, gain, state, conv0, prev, w, j, bsz, t_len):
    assert t_len >= 8 and t_len % HALO == 0
    conv0p = jnp.pad(conv0, ((0, 0), (8 - (A_CONV - 1), 0), (0, 0)))
    q, k, v, xc, z, gcol, grow, tail = _mlstm_front(x.reshape(bsz, t_len, -1), gain, conv0p, w, j)
    hn, c_st, n_st, m_st = _mlstm_rec(q, k, v, gcol, grow, w["g_head"][j], state, j, prev)
    flat = lambda a: a.reshape(bsz * t_len, A_INNER)
    return (flat(hn), flat(xc), flat(z), w["skip"][j]), (c_st, n_st, m_st), tail[:, 8 - (A_CONV - 1):]


def _sb_layer(x, gain, past, w, j, bsz, t_len, q_off):
    d = x.shape[1]
    q, k, kb, v, vb = _proj(x, gain, [(w["wqkv"], j)],
                            [(0, 0, d, BF16), (0, d, 2 * d, F32), (0, d, 2 * d, BF16),
                             (0, 2 * d, 3 * d, F32), (0, 2 * d, 3 * d, BF16)], name="sb_qkv")
    k3 = kb.reshape(bsz, t_len, d)
    v3 = vb.reshape(bsz, t_len, d)
    if past is not None:
        k3 = jnp.concatenate([past[0].reshape(bsz, -1, d).astype(BF16), k3], axis=1)
        v3 = jnp.concatenate([past[1].reshape(bsz, -1, d).astype(BF16), v3], axis=1)
    tk_pad = -(-k3.shape[1] // LANES) * LANES
    k3 = _pad_rows(k3, tk_pad)
    v3 = _pad_rows(v3, tk_pad)
    o = _sb_attn(q.reshape(bsz, t_len, d), k3, v3, w["u"], q_off)
    return o.reshape(bsz * t_len, d), (k.reshape(bsz, t_len, B_HEADS, B_DH), v.reshape(bsz, t_len, B_HEADS, B_DH))


def _rope_tables(t_len, q_off):
    half = C_ROPE // 2
    inv = ROPE_THETA ** (-jnp.arange(half, dtype=F32) / half)
    ang = (jnp.arange(t_len, dtype=F32) + q_off)[:, None] * inv[None, :]
    cos, sin = jnp.cos(ang), jnp.sin(ang)
    ones = jnp.ones((t_len, HALF), F32)
    zeros = jnp.zeros((t_len, HALF), F32)
    tail = LANES - HALF - C_ROPE
    cos_t = jnp.concatenate([ones, cos, cos, ones[:, :tail]], axis=1)
    sin_s = jnp.concatenate([zeros, -sin, sin, zeros[:, :tail]], axis=1)
    return cos_t, sin_s


def _mla_layer(x, gain, past, w, j, bsz, t_len, q_off):
    cos_t, sin_s = _rope_tables(t_len, q_off)
    outs = _mla_front(x, gain, w, j, cos_t, sin_s, t_len, fuse_kv=past is None)
    q, ckv, krp, krpb = outs[:4]
    ckv3 = ckv.reshape(bsz, t_len, C_KV_RANK)
    kr_new = krp.reshape(bsz, t_len, LANES)[:, :, HALF:HALF + C_ROPE]
    krpb3 = krpb.reshape(bsz, t_len, LANES)
    if past is None:
        assert t_len % LANES == 0
        n_valid = tk_pad = t_len
        kv, krp_all = outs[4], krpb3
    else:
        ckv_all = jnp.concatenate([past[0], ckv3], axis=1)
        kr_pad = jnp.pad(past[1], ((0, 0), (0, 0), (HALF, LANES - HALF - C_ROPE))).astype(BF16)
        n_valid = ckv_all.shape[1]
        tk_pad = -(-n_valid // LANES) * LANES
        krp_all = _pad_rows(jnp.concatenate([kr_pad, krpb3], axis=1), tk_pad)
        (kv,) = _proj(_pad_rows(ckv_all, tk_pad).reshape(bsz * tk_pad, C_KV_RANK), None, [(w["ukv"], j)],
                      [(0, 0, C_HEADS * LANES, BF16)], name="mla_kv_up")
    o = _mla_attn(q.reshape(bsz, t_len, C_HEADS * LANES), kv.reshape(bsz, tk_pad, C_HEADS * LANES),
                  krp_all, q_off, n_valid)
    return o.reshape(bsz * t_len, C_HEADS * C_VDIM), (ckv3, kr_new)


def _trunk(x3, q_off, st_a, past_b, past_c, wts):
    bsz, t_len, d = x3.shape
    x = x3.reshape(bsz * t_len, d)
    a_states, a_convs, new_b, new_c = None, [], [], []
    state = None
    if st_a is not None:
        m_rep = jnp.broadcast_to(st_a[2][..., None], st_a[2].shape + (LANES,))
        state = (st_a[0], st_a[1], m_rep)
    for i in range(DEPTH):
        j = i // 3
        if i % 3 == 0:
            conv0 = jnp.zeros((bsz, A_CONV - 1, A_INNER), F32) if st_a is None else st_a[3][j]
            mixed, a_states, conv_new = _mlstm_layer(x, wts["norm_mix"][i], state, conv0, a_states,
                                                     wts["a"], j, bsz, t_len)
            a_convs.append(conv_new)
            wo = wts["a"]["down"]
        elif i % 3 == 1:
            past = None if past_b is None else (past_b[0][j], past_b[1][j])
            mixed, kv = _sb_layer(x, wts["norm_mix"][i], past, wts["b"], j, bsz, t_len, q_off)
            new_b.append(kv)
            wo = wts["b"]["wo"]
        else:
            past = None if past_c is None else (past_c[0][j], past_c[1][j])
            mixed, lat = _mla_layer(x, wts["norm_mix"][i], past, wts["c"], j, bsz, t_len, q_off)
            new_c.append(lat)
            wo = wts["c"]["wo"]
        final = wts["norm_final"] if i == DEPTH - 1 else None
        x = _mixer_out_mlp(x, mixed, wo, j, wts["norm_mlp"][i], wts["ff1"], wts["ff2"], i, final_gain=final)
    stack = lambda items, idx: jnp.stack([it[idx] for it in items])
    return (x.reshape(bsz, t_len, d),
            a_states[0], a_states[1], a_states[2][..., 0], jnp.stack(a_convs),
            stack(new_b, 0), stack(new_b, 1), stack(new_c, 0), stack(new_c, 1))


def _prep_weights(norm_mix, norm_mlp, norm_final, a_w_up, a_conv_w, a_conv_b, a_w_q, a_w_k, a_w_v,
                  a_w_gate, a_b_i, a_b_f, a_g_head, a_skip, a_w_down, b_w_qkv, b_w_o, c_w_dq, c_g_q,
                  c_w_uq, c_w_dkv, c_g_kv, c_w_ukv, c_w_o, w_ff1, w_ff2):
    bf = lambda a: a.astype(BF16)
    n_a = a_w_up.shape[0]
    wg = a_w_gate.reshape(n_a, 3, A_HEADS, A_DH, 2 * A_HEADS)
    wg = jnp.pad(wg, ((0, 0),) * 4 + ((0, LANES - 2 * A_HEADS),))
    gb = jnp.pad(jnp.concatenate([a_b_i, a_b_f], axis=1), ((0, 0), (0, LANES - 2 * A_HEADS)))
    a = dict(up=bf(a_w_up), conv_w=a_conv_w, conv_b=a_conv_b.reshape(n_a, 1, A_INNER), wq=bf(a_w_q),
             wk=bf(a_w_k), wv=bf(a_w_v), wg=bf(wg), gb=gb.reshape(n_a, 1, LANES),
             g_head=a_g_head.reshape(n_a, 1, A_INNER), skip=a_skip.reshape(n_a, 1, A_INNER), down=bf(a_w_down))
    d = b_w_o.shape[1]
    qscale = jnp.where(jnp.arange(3 * d) < d, (B_DH ** -0.5) * LOG2E, 1.0).astype(F32)
    jj = lax.broadcasted_iota(jnp.int32, (2 * LANES, 2 * LANES), 0) % LANES
    ss = lax.broadcasted_iota(jnp.int32, (2 * LANES, 2 * LANES), 1)
    b = dict(wqkv=bf(b_w_qkv * qscale), wo=bf(b_w_o), u=-((ss >= LANES) | (jj > ss)).astype(BF16))
    n_c = c_w_dq.shape[0]
    uq = c_w_uq.reshape(n_c, -1, C_HEADS, C_NOPE + C_ROPE)
    uq = jnp.pad(uq, ((0, 0),) * 3 + ((0, LANES - C_NOPE - C_ROPE),)).reshape(n_c, -1, C_HEADS * LANES)
    dkv = jnp.concatenate([c_w_dkv[..., :C_KV_RANK], jnp.zeros(c_w_dkv.shape[:2] + (HALF,), F32),
                           c_w_dkv[..., C_KV_RANK:],
                           jnp.zeros(c_w_dkv.shape[:2] + (LANES - HALF - C_ROPE,), F32)], axis=-1)
    c = dict(dq=bf(c_w_dq), dkv=bf(dkv), g_q=c_g_q.reshape(n_c, 1, -1), g_kv=c_g_kv.reshape(n_c, 1, -1),
             uq=bf(uq), ukv=bf(c_w_ukv.reshape(n_c, C_KV_RANK, C_HEADS * LANES)), wo=bf(c_w_o))
    return dict(norm_mix=norm_mix, norm_mlp=norm_mlp, norm_final=norm_final, a=a, b=b, c=c,
                ff1=bf(w_ff1), ff2=bf(w_ff2))


def kernel(x_prompt, x_sample, state_mlstm_C, state_mlstm_n, state_mlstm_m, state_mlstm_conv, cache_sb_k, cache_sb_v, cache_mla_ckv, cache_mla_krope, norm_mix, norm_mlp, norm_final, a_w_up, a_conv_w, a_conv_b, a_w_q, a_w_k, a_w_v, a_w_gate, a_b_i, a_b_f, a_g_head, a_skip, a_w_down, b_w_qkv, b_w_o, c_w_dq, c_g_q, c_w_uq, c_w_dkv, c_g_kv, c_w_ukv, c_w_o, w_ff1, w_ff2):
    wts = _prep_weights(norm_mix, norm_mlp, norm_final, a_w_up, a_conv_w, a_conv_b, a_w_q, a_w_k, a_w_v,
                        a_w_gate, a_b_i, a_b_f, a_g_head, a_skip, a_w_down, b_w_qkv, b_w_o, c_w_dq, c_g_q,
                        c_w_uq, c_w_dkv, c_g_kv, c_w_ukv, c_w_o, w_ff1, w_ff2)
    outs_p = _trunk(x_prompt, 0, None, None, None, wts)
    past_len = cache_sb_k.shape[2]
    outs_s = _trunk(x_sample, past_len,
                    (state_mlstm_C, state_mlstm_n, state_mlstm_m, state_mlstm_conv),
                    (cache_sb_k, cache_sb_v), (cache_mla_ckv, cache_mla_krope), wts)
    y_p, rest_p = outs_p[0], outs_p[1:]
    y_s, rest_s = outs_s[0], outs_s[1:]
    return (y_p, y_s) + tuple(rest_p) + tuple(rest_s)
```

```python
import functools

import jax
import jax.numpy as jnp
from jax import lax
from jax.experimental import pallas as pl
from jax.experimental.pallas import tpu as pltpu

F32 = jnp.float32
BF16 = jnp.bfloat16

EPS = 1e-6
DEPTH = 4
CHUNK = 64
A_HEADS = 4
A_DH = 512
A_INNER = A_HEADS * A_DH
A_CONV = 4
B_HEADS = 16
B_DH = 64
C_HEADS = 16
C_NOPE = 64
C_ROPE = 32
C_VDIM = 64
C_KV_RANK = 256
C_SCALE = (C_NOPE + C_ROPE) ** -0.5
ROPE_THETA = 10000.0

LANES = 128
MXU_DIM = 256
HALF = LANES // 2
NEG_BIG = -1e30
LOG2E = 1.4426950408889634
VMEM_LIMIT_BYTES = 56 * 1024 * 1024


def _cparams(*sem):
    return pltpu.CompilerParams(dimension_semantics=sem, vmem_limit_bytes=VMEM_LIMIT_BYTES)


def _tile(n, pref):
    if n <= pref:
        return n
    t = pref
    while n % t:
        t //= 2
    return t


def _const_spec(shape, layer=None):
    nd = len(shape)
    if layer is None:
        return pl.BlockSpec(shape, lambda *_: (0,) * nd)
    return pl.BlockSpec((None,) + tuple(shape[1:]), lambda *_: (layer,) + (0,) * (nd - 1))


def _rms(x, g):
    return x * lax.rsqrt(jnp.mean(x * x, axis=-1, keepdims=True) + EPS) * g


def _log_sigmoid(x):
    return jnp.minimum(x, 0.0) - jnp.log1p(jnp.exp(-jnp.abs(x)))


def _proj_kernel(*refs, n_w, outs, has_norm, has_res):
    refs = list(refs)
    x_ref = refs.pop(0)
    g_ref = refs.pop(0) if has_norm else None
    w_refs = [refs.pop(0) for _ in range(n_w)]
    r_ref = refs.pop(0) if has_res else None
    if has_norm:
        h = _rms(x_ref[...].astype(F32), g_ref[...]).astype(BF16)
    else:
        h = x_ref[...].astype(BF16)
    groups = {}
    for o_ref, (wi, c0, c1, _) in zip(refs, outs):
        groups.setdefault((wi, c0, c1), []).append(o_ref)
    for gi, ((wi, c0, c1), dests) in enumerate(groups.items()):
        n = c1 - c0
        c = n if n <= 512 else 512
        for n0 in range(0, n, c):
            y = jnp.dot(h, w_refs[wi][:, c0 + n0:c0 + n0 + c], preferred_element_type=F32)
            if has_res and gi == 0:
                y = y + r_ref[:, n0:n0 + c]
            for o_ref in dests:
                o_ref[:, n0:n0 + c] = y.astype(o_ref.dtype)


def _proj(x, gain, weights, outs, residual=None, tm_pref=512, name="proj"):
    m, k = x.shape
    tm = _tile(m, tm_pref)
    in_specs = [pl.BlockSpec((tm, k), lambda i: (i, 0))]
    args = [x]
    if gain is not None:
        in_specs.append(_const_spec((1, k)))
        args.append(gain.reshape(1, k).astype(F32))
    for w, layer in weights:
        in_specs.append(_const_spec(w.shape, layer))
        args.append(w)
    if residual is not None:
        in_specs.append(pl.BlockSpec((tm, residual.shape[1]), lambda i: (i, 0)))
        args.append(residual)
    out_shape = [jax.ShapeDtypeStruct((m, c1 - c0), dt) for _, c0, c1, dt in outs]
    out_specs = [pl.BlockSpec((tm, c1 - c0), lambda i: (i, 0)) for _, c0, c1, _ in outs]
    kern = functools.partial(_proj_kernel, n_w=len(weights), outs=tuple(outs),
                             has_norm=gain is not None, has_res=residual is not None)
    return pl.pallas_call(kern, name=name, grid=(m // tm,), in_specs=in_specs, out_specs=out_specs,
                          out_shape=out_shape, compiler_params=_cparams("parallel"))(*args)


def _mlp_kernel(*refs, pre, final, tf):
    refs = list(refs)
    x_ref = refs.pop(0)
    if pre == "mlstm":
        hn_ref, xc_ref, z_ref, skip_ref = (refs.pop(0) for _ in range(4))
        z = z_ref[...].astype(F32)
        a = (hn_ref[...].astype(F32) + skip_ref[...] * xc_ref[...].astype(F32)) * (1.0 / (1.0 + jnp.exp(-z)))
        a = a.astype(BF16)
    else:
        a = refs.pop(0)[...]
    wo_ref, g_ref, w1_ref, w2_ref = (refs.pop(0) for _ in range(4))
    gf_ref = refs.pop(0) if final else None
    (o_ref,) = refs
    x = x_ref[...] + jnp.dot(a, wo_ref[...], preferred_element_type=F32)
    h = _rms(x, g_ref[...]).astype(BF16)
    acc = x
    for f0 in range(0, w1_ref.shape[1], tf):
        u = jnp.dot(h, w1_ref[:, f0:f0 + tf], preferred_element_type=F32)
        u = jnp.maximum(u, 0.0)
        u = u * u
        acc = acc + jnp.dot(u.astype(BF16), w2_ref[f0:f0 + tf, :], preferred_element_type=F32)
    if final:
        acc = _rms(acc, gf_ref[...])
    o_ref[...] = acc


def _mixer_out_mlp(x, mixed, wo, wo_layer, gain, w1, w2, layer, final_gain=None, tm_pref=512, tf_pref=1024):
    m, d = x.shape
    dff = w1.shape[2]
    tm = _tile(m, tm_pref)
    tf = _tile(dff, tf_pref)
    rows = lambda n: pl.BlockSpec((tm, n), lambda i: (i, 0))
    resident = lambda arr, lyr: pl.BlockSpec((None,) + arr.shape[1:], lambda i: (lyr, 0, 0),
                                             pipeline_mode=pl.Buffered(1))
    in_specs, args = [rows(d)], [x]
    if isinstance(mixed, tuple):
        hn, xc, z, skip = mixed
        pre = "mlstm"
        in_specs += [rows(hn.shape[1]), rows(xc.shape[1]), rows(z.shape[1]), _const_spec(skip.shape)]
        args += [hn, xc, z, skip]
    else:
        pre = "proj"
        in_specs.append(rows(mixed.shape[1]))
        args.append(mixed)
    in_specs += [resident(wo, wo_layer), _const_spec((1, d)), resident(w1, layer), resident(w2, layer)]
    args += [wo, gain.reshape(1, d), w1, w2]
    if final_gain is not None:
        in_specs.append(_const_spec((1, d)))
        args.append(final_gain.reshape(1, d))
    return pl.pallas_call(
        functools.partial(_mlp_kernel, pre=pre, final=final_gain is not None, tf=tf), name="out_mlp",
        grid=(m // tm,), in_specs=in_specs, out_specs=rows(d),
        out_shape=jax.ShapeDtypeStruct((m, d), F32),
        compiler_params=_cparams("parallel"))(*args)


HALO = 16


def _mlstm_front_kernel(x_ref, xh_ref, conv0_ref, gn_ref, wup_ref, cw_ref, cb_ref, wq_ref, wk_ref, wv_ref,
                        wg_ref, gb_ref, q_ref, k_ref, v_ref, xc_ref, z_ref, gcol_ref, grow_ref, tail_ref,
                        xp_ref, *, tt):
    t = pl.program_id(1)
    hn = _rms(x_ref[0], gn_ref[...]).astype(BF16)
    hh = _rms(xh_ref[0], gn_ref[...]).astype(BF16)
    xms = []
    for h in range(A_HEADS):
        sl = slice(h * A_DH, (h + 1) * A_DH)
        xm = jnp.dot(hn, wup_ref[:, sl], preferred_element_type=F32)
        xm_prev = jnp.dot(hh, wup_ref[:, sl], preferred_element_type=F32)[HALO - 8:]
        xp_ref[0:8, sl] = jnp.where(t == 0, conv0_ref[0, :, sl], xm_prev)
        xp_ref[8:8 + tt, sl] = xm
        xms.append(xm)
    for h in range(A_HEADS):
        z_ref[0, :, h * A_DH:(h + 1) * A_DH] = jnp.dot(
            hn, wup_ref[:, A_INNER + h * A_DH:A_INNER + (h + 1) * A_DH], preferred_element_type=F32).astype(BF16)
    g = jnp.zeros((tt, LANES), F32)
    for h in range(A_HEADS):
        sl = slice(h * A_DH, (h + 1) * A_DH)
        xm = xms[h]
        xc = cb_ref[:, sl] + xm * cw_ref[A_CONV - 1:A_CONV, sl]
        for j in range(A_CONV - 1):
            xc = xc + xp_ref[5 + j:5 + j + tt, sl] * cw_ref[j:j + 1, sl]
        xc = xc / (1.0 + jnp.exp(-xc))
        xch = xc.astype(BF16)
        xc_ref[0, :, sl] = xch
        qh = jnp.dot(xch, wq_ref[h], preferred_element_type=F32).astype(BF16)
        kh = (jnp.dot(xch, wk_ref[h], preferred_element_type=F32) * (A_DH ** -0.5)).astype(BF16)
        vh = jnp.dot(xm.astype(BF16), wv_ref[h], preferred_element_type=F32).astype(BF16)
        q_ref[0, :, sl] = qh
        k_ref[0, :, sl] = kh
        v_ref[0, :, sl] = vh
        g = g + jnp.dot(qh, wg_ref[0, h], preferred_element_type=F32)
        g = g + jnp.dot(kh, wg_ref[1, h], preferred_element_type=F32)
        g = g + jnp.dot(vh, wg_ref[2, h], preferred_element_type=F32)
    tail_ref[0] = xp_ref[tt:tt + 8, :]
    g = g + gb_ref[...]
    lane = lax.broadcasted_iota(jnp.int32, (tt, LANES), 1)
    g = jnp.where(lane < A_HEADS, g, _log_sigmoid(g))
    gcol_ref[0] = g
    sel = (lax.broadcasted_iota(jnp.int32, (8, LANES), 0) ==
           lax.broadcasted_iota(jnp.int32, (8, LANES), 1)).astype(BF16)
    grow = jnp.zeros((8, tt), F32)
    rem = g
    for _ in range(3):
        part = rem.astype(BF16)
        grow = grow + lax.dot_general(sel, part, (((1,), (1,)), ((), ())), preferred_element_type=F32)
        rem = rem - part.astype(F32)
    grow_ref[0] = grow


def _mlstm_front(x3, gain, conv0p, w, layer, tt_pref=512):
    bsz, t_len, d = x3.shape
    tt = _tile(t_len, tt_pref)
    nhalo = tt // HALO
    resident = lambda arr: pl.BlockSpec((None,) + arr.shape[1:], lambda b, t, nd=arr.ndim: (layer,) + (0,) * (nd - 1),
                                        pipeline_mode=pl.Buffered(1))
    act = lambda dt: jax.ShapeDtypeStruct((bsz, t_len, A_INNER), dt)
    act_spec = pl.BlockSpec((1, tt, A_INNER), lambda b, t: (b, t, 0))
    cw, cb, gb = w["conv_w"][layer], w["conv_b"][layer], w["gb"][layer]
    return pl.pallas_call(
        functools.partial(_mlstm_front_kernel, tt=tt), name="mlstm_front",
        grid=(bsz, t_len // tt),
        in_specs=[pl.BlockSpec((1, tt, d), lambda b, t: (b, t, 0)),
                  pl.BlockSpec((1, HALO, d), lambda b, t: (b, jnp.maximum(t * nhalo - 1, 0), 0)),
                  pl.BlockSpec((1, 8, A_INNER), lambda b, t: (b, 0, 0)),
                  _const_spec((1, d)), resident(w["up"]),
                  _const_spec(cw.shape), _const_spec(cb.shape), resident(w["wq"]), resident(w["wk"]),
                  resident(w["wv"]), resident(w["wg"]), _const_spec(gb.shape)],
        out_specs=[act_spec, act_spec, act_spec, act_spec, act_spec,
                   pl.BlockSpec((1, tt, LANES), lambda b, t: (b, t, 0)),
                   pl.BlockSpec((1, 8, tt), lambda b, t: (b, 0, t)),
                   pl.BlockSpec((1, 8, A_INNER), lambda b, t: (b, 0, 0))],
        out_shape=[act(BF16), act(BF16), act(BF16), act(BF16), act(BF16),
                   jax.ShapeDtypeStruct((bsz, t_len, LANES), F32),
                   jax.ShapeDtypeStruct((bsz, 8, t_len), F32),
                   jax.ShapeDtypeStruct((bsz, 8, A_INNER), F32)],
        scratch_shapes=[pltpu.VMEM((tt + 8, A_INNER), F32)],
        compiler_params=_cparams("parallel", "arbitrary"))(
            x3, x3, conv0p, gain.reshape(1, d), w["up"], cw, cb, w["wq"], w["wk"], w["wv"], w["wg"], gb)


def _mlstm_rec_kernel(*refs, lc, has_state, n_prev, n_chunks):
    refs = list(refs)
    q_ref, k_ref, v_ref, gcol_ref, grow_ref, gh_ref = (refs.pop(0) for _ in range(6))
    init_refs = [refs.pop(0) for _ in range(3)] if has_state else None
    prev_refs = [refs.pop(0) for _ in range(3)] if n_prev else None
    hn_ref, cs_ref, ns_ref, ms_ref = refs
    c_ref, n_ref, m_ref = cs_ref.at[n_prev], ns_ref.at[n_prev], ms_ref.at[n_prev]
    c_idx = pl.program_id(1)

    @pl.when(c_idx == 0)
    def _():
        for st_ref, i_ref in zip((c_ref, n_ref, m_ref), init_refs or (None,) * 3):
            st_ref[...] = jnp.zeros_like(st_ref) if i_ref is None else i_ref[...]
        if n_prev:
            for out_ref, p_ref in zip((cs_ref, ns_ref, ms_ref), prev_refs):
                out_ref[0:n_prev] = p_ref[...]

    row = lax.broadcasted_iota(jnp.int32, (lc, lc), 0)
    col = lax.broadcasted_iota(jnp.int32, (lc, lc), 1)
    causal = col <= row
    for ci, h in [(ci, h) for ci in range(n_chunks) for h in range(A_HEADS)]:
        rows = slice(ci * lc, (ci + 1) * lc)
        sl = slice(h * A_DH, (h + 1) * A_DH)
        qh = q_ref[0, rows, sl]
        kh = k_ref[0, rows, sl]
        vh = v_ref[0, rows, sl]
        li_c = gcol_ref[0, rows, h:h + 1]
        lf_c = gcol_ref[0, rows, A_HEADS + h:A_HEADS + h + 1]
        li_r = grow_ref[0, h:h + 1, rows]
        lf_r = grow_ref[0, A_HEADS + h:A_HEADS + h + 1, rows]
        b_c = jnp.sum(jnp.where(causal, lf_r, 0.0), axis=1, keepdims=True)
        b_r = jnp.sum(jnp.where(row <= col, lf_c, 0.0), axis=0, keepdims=True)
        c_old = c_ref[0, h]
        n_old = n_ref[0, h:h + 1, :]
        m_old = m_ref[0, h:h + 1, 0:1]
        dmat = jnp.where(causal, b_c - b_r + li_r, NEG_BIG)
        inter = b_c + m_old
        m_t = jnp.maximum(inter, jnp.max(dmat, axis=1, keepdims=True))
        w_inter = jnp.exp(inter - m_t)
        s = lax.dot_general(qh, kh, (((1,), (1,)), ((), ())), preferred_element_type=F32)
        s = s * jnp.exp(dmat - m_t)
        num = w_inter * jnp.dot(qh, c_old.astype(BF16), preferred_element_type=F32)
        num = num + jnp.dot(s.astype(BF16), vh, preferred_element_type=F32)
        nq = w_inter * jnp.sum(qh.astype(F32) * n_old, axis=1, keepdims=True)
        nq = nq + jnp.sum(s, axis=1, keepdims=True)
        hv = num / jnp.maximum(jnp.abs(nq), jnp.exp(-m_t))
        mu = jnp.mean(hv, axis=1, keepdims=True)
        hc = hv - mu
        var = jnp.mean(hc * hc, axis=1, keepdims=True)
        hn_ref[0, rows, sl] = (hc * lax.rsqrt(var + EPS) * gh_ref[:, sl]).astype(hn_ref.dtype)
        m_new = m_t[lc - 1:lc, :]
        b_last = b_c[lc - 1:lc, :]
        g_state = jnp.exp(b_last + m_old - m_new)
        g_tok = jnp.exp(b_last - b_c + li_c - m_new)
        kw = kh.astype(F32) * g_tok
        c_ref[0, h] = g_state * c_old + jnp.dot(kw.T.astype(BF16), vh, preferred_element_type=F32)
        n_ref[0, h:h + 1, :] = g_state * n_old + jnp.sum(kw, axis=0, keepdims=True)
        m_ref[0, h:h + 1, :] = jnp.broadcast_to(m_new, (1, LANES))


def _mlstm_rec(q, k, v, gcol, grow, g_head, state, layer, prev, lc_pref=256, chunks_pref=2):
    bsz, t_len, _ = q.shape
    lc = _tile(t_len, lc_pref)
    n_chunks = _tile(t_len // lc, chunks_pref)
    ts = lc * n_chunks
    n_prev = 0 if prev is None else prev[0].shape[0]
    act_spec = pl.BlockSpec((1, ts, A_INNER), lambda b, c: (b, c, 0))
    st_shapes = [(1, A_HEADS, A_DH, A_DH), (1, A_HEADS, A_DH), (1, A_HEADS, LANES)]
    stacked = lambda n: [pl.BlockSpec((n,) + shp, lambda b, c, nd=len(shp): (0, b) + (0,) * (nd - 1))
                         for shp in st_shapes]
    in_specs = [act_spec, act_spec, act_spec,
                pl.BlockSpec((1, ts, LANES), lambda b, c: (b, c, 0)),
                pl.BlockSpec((1, 8, ts), lambda b, c: (b, 0, c)),
                _const_spec((1, A_INNER))]
    args = [q, k, v, gcol, grow, g_head]
    if state is not None:
        in_specs += [pl.BlockSpec((None,) + shp, lambda b, c, nd=len(shp): (layer, b) + (0,) * (nd - 1))
                     for shp in st_shapes]
        args += list(state)
    if n_prev:
        in_specs += stacked(n_prev)
        args += list(prev)
    return pl.pallas_call(
        functools.partial(_mlstm_rec_kernel, lc=lc, has_state=state is not None, n_prev=n_prev,
                          n_chunks=n_chunks),
        name="mlstm_rec", grid=(bsz, t_len // ts), in_specs=in_specs,
        out_specs=[act_spec] + stacked(n_prev + 1),
        out_shape=[jax.ShapeDtypeStruct((bsz, t_len, A_INNER), BF16)] +
                  [jax.ShapeDtypeStruct((n_prev + 1, bsz) + shp[1:], F32) for shp in st_shapes],
        compiler_params=_cparams("parallel", "arbitrary"))(*args)


def _sb_attn_kernel(q_ref, k_ref, v_ref, u_ref, o_ref, *, tq, tk, q_off, groups):
    lane = lax.broadcasted_iota(jnp.int32, (tq, LANES), 1)
    u = u_ref[...]
    n_carry = 4 * groups
    half = tq // 2

    def block(qms, qmin, row0, nrows, start, nkeys, carry, masked):
        nsub = nkeys // LANES
        if masked:
            qpos = qmin + row0 + lax.broadcasted_iota(jnp.int32, (nrows, nkeys), 0)
            before = (lax.broadcasted_iota(jnp.int32, (nrows, nkeys), 1) + start) < qpos
        new = []
        for hd in range(2 * groups):
            sl = slice((hd // 2) * LANES, (hd // 2 + 1) * LANES)
            k2 = k_ref[0, start:start + nkeys, sl]
            v2 = v_ref[0, start:start + nkeys, sl]
            acc, run = carry[2 * hd], carry[2 * hd + 1]
            z = lax.dot_general(qms[hd][row0:row0 + nrows], k2, (((1,), (1,)), ((), ())),
                                preferred_element_type=F32)
            sp = jnp.maximum(z, 0.0) + jnp.log(1.0 + jnp.exp2(-jnp.abs(z))) * LOG2E
            nlf = jnp.where(before, sp, 0.0) if masked else sp
            hi = nlf.astype(BF16)
            lo = (nlf - hi.astype(F32)).astype(BF16)
            pieces = [jnp.concatenate([hi[:, sb * LANES:(sb + 1) * LANES], lo[:, sb * LANES:(sb + 1) * LANES]], axis=1)
                      for sb in range(nsub)]
            if nsub > 1 and nrows < MXU_DIM:
                r_all = jnp.dot(jnp.concatenate(pieces, axis=0), u, preferred_element_type=F32)
                rs = [r_all[sb * nrows:(sb + 1) * nrows] for sb in range(nsub)]
            else:
                rs = [jnp.dot(p, u, preferred_element_type=F32) for p in pieces]
            rests = [None] * nsub
            for sb in reversed(range(nsub)):
                rests[sb] = rs[sb][:, :LANES] + run
                run = run + rs[sb][:, LANES:]
            rest = rests[0] if nsub == 1 else jnp.concatenate(rests, axis=1)
            att = jnp.exp2(z - sp + rest)
            if masked:
                att = jnp.where(before, att, 0.0)
            acc = acc + jnp.dot(att.astype(BF16), v2, preferred_element_type=F32)
            new += [acc, run]
        return tuple(new)

    for i in range(q_ref.shape[1] // tq):
        rows = slice(i * tq, (i + 1) * tq)
        qmin = q_off + i * tq
        nkb = (qmin + tq - 2) // tk + 1
        n_diag = nkb - qmin // tk
        qms = []
        for g in range(groups):
            q2 = q_ref[0, rows, g * LANES:(g + 1) * LANES]
            qms += [jnp.where((lane >= HALF * e) & (lane < HALF * (e + 1)), q2, jnp.zeros_like(q2))
                    for e in range(2)]
        if tq == tk and qmin % tk == 0 and half % LANES == 0:
            zeros = jnp.zeros((half, LANES), F32)
            top = block(qms, qmin, 0, half, qmin, half, (zeros,) * n_carry, True)
            bot = block(qms, qmin, half, half, qmin, tk, (zeros,) * n_carry, True)
            res = tuple(jnp.concatenate([a, b], axis=0) for a, b in zip(top, bot))
            n_diag = 1
        else:
            res = (jnp.zeros((tq, LANES), F32),) * n_carry
            for j in range(n_diag):
                res = block(qms, qmin, 0, tq, (nkb - 1 - j) * tk, tk, res, True)
        for j in range(n_diag, nkb):
            res = block(qms, qmin, 0, tq, (nkb - 1 - j) * tk, tk, res, False)
        for g in range(groups):
            o_ref[0, rows, g * LANES:(g + 1) * LANES] = jnp.where(
                lane < HALF, res[4 * g], res[4 * g + 2]).astype(o_ref.dtype)


def _key_tile(tk_len):
    return tk_len if tk_len <= 10 * LANES else 4 * LANES


def _attn_groups(tq_len):
    return 4 if tq_len < LANES else 1


def _sb_attn(q, k_all, v_all, u, q_off, tq_pref=512):
    bsz, tq_len, d = q.shape
    groups = _attn_groups(tq_len)
    tk_len = k_all.shape[1]
    tq = _tile(tq_len, tq_pref)
    tk = _key_tile(tk_len)
    gw = groups * LANES
    assert tk_len % tk == 0 and (q_off + tq_len - 2) // tk < tk_len // tk and d % gw == 0
    return pl.pallas_call(
        functools.partial(_sb_attn_kernel, tq=tq, tk=tk, q_off=q_off, groups=groups), name="sb_attn",
        grid=(bsz, d // gw),
        in_specs=[pl.BlockSpec((1, tq_len, gw), lambda b, h: (b, 0, h)),
                  pl.BlockSpec((1, tk_len, gw), lambda b, h: (b, 0, h)),
                  pl.BlockSpec((1, tk_len, gw), lambda b, h: (b, 0, h)),
                  _const_spec(u.shape)],
        out_specs=pl.BlockSpec((1, tq_len, gw), lambda b, h: (b, 0, h)),
        out_shape=jax.ShapeDtypeStruct((bsz, tq_len, d), BF16),
        compiler_params=_cparams("parallel", "parallel"))(q, k_all, v_all, u)


def _rope128(x, cos_t, sin_s, lane):
    rot = jnp.where(lane < HALF + C_ROPE // 2, pltpu.roll(x, LANES - C_ROPE // 2, 1),
                    pltpu.roll(x, C_ROPE // 2, 1))
    return x * cos_t + rot * sin_s


def _mla_front_kernel(x_ref, g_ref, wdq_ref, wdkv_ref, gq_ref, gkv_ref, wuq_ref, cos_ref, sin_ref, *rest,
                      fuse_kv):
    if fuse_kv:
        wukv_ref, q_ref, ckvn_ref, krp_ref, krpb_ref, kv_ref = rest
    else:
        q_ref, ckvn_ref, krp_ref, krpb_ref = rest
    tm = x_ref.shape[0]
    lane = lax.broadcasted_iota(jnp.int32, (tm, LANES), 1)
    cos_t = cos_ref[...]
    sin_s = sin_ref[...]
    h = _rms(x_ref[...], g_ref[...]).astype(BF16)
    cq = _rms(jnp.dot(h, wdq_ref[...], preferred_element_type=F32), gq_ref[...]).astype(BF16)
    kva = jnp.dot(h, wdkv_ref[...], preferred_element_type=F32)
    for hd in range(C_HEADS):
        sl = slice(hd * LANES, (hd + 1) * LANES)
        qh = jnp.dot(cq, wuq_ref[:, sl], preferred_element_type=F32)
        q_ref[:, sl] = (_rope128(qh, cos_t, sin_s, lane) * (C_SCALE * LOG2E)).astype(BF16)
    ckv = _rms(kva[:, :C_KV_RANK], gkv_ref[...])
    ckvn_ref[...] = ckv
    krp = _rope128(kva[:, C_KV_RANK:], cos_t, sin_s, lane)
    krp_ref[...] = krp
    krpb_ref[...] = krp.astype(BF16)
    if fuse_kv:
        ckv_b = ckv.astype(BF16)
        for n0 in range(0, kv_ref.shape[1], 512):
            kv_ref[:, n0:n0 + 512] = jnp.dot(ckv_b, wukv_ref[:, n0:n0 + 512],
                                             preferred_element_type=F32).astype(BF16)


def _mla_front(x, gain, w, layer, cos_t, sin_s, t_len, fuse_kv, tm_pref=512):
    m, d = x.shape
    tm = _tile(t_len, tm_pref)
    nt = t_len // tm
    rows = lambda n: pl.BlockSpec((tm, n), lambda i: (i, 0))
    tab = pl.BlockSpec((tm, LANES), lambda i: (i % nt, 0))
    in_specs = [rows(d), _const_spec((1, d)), _const_spec(w["dq"].shape, layer), _const_spec(w["dkv"].shape, layer),
                _const_spec(w["g_q"].shape[1:]), _const_spec(w["g_kv"].shape[1:]), _const_spec(w["uq"].shape, layer),
                tab, tab]
    args = [x, gain.reshape(1, d), w["dq"], w["dkv"], w["g_q"][layer], w["g_kv"][layer], w["uq"], cos_t, sin_s]
    out_specs = [rows(C_HEADS * LANES), rows(C_KV_RANK), rows(LANES), rows(LANES)]
    out_shape = [jax.ShapeDtypeStruct((m, C_HEADS * LANES), BF16), jax.ShapeDtypeStruct((m, C_KV_RANK), F32),
                 jax.ShapeDtypeStruct((m, LANES), F32), jax.ShapeDtypeStruct((m, LANES), BF16)]
    if fuse_kv:
        in_specs.append(_const_spec(w["ukv"].shape, layer))
        args.append(w["ukv"])
        out_specs.append(rows(C_HEADS * LANES))
        out_shape.append(jax.ShapeDtypeStruct((m, C_HEADS * LANES), BF16))
    return pl.pallas_call(
        functools.partial(_mla_front_kernel, fuse_kv=fuse_kv), name="mla_front", grid=(m // tm,),
        in_specs=in_specs, out_specs=out_specs, out_shape=out_shape,
        compiler_params=_cparams("parallel"))(*args)


def _mla_attn_kernel(q_ref, kv_ref, kr_ref, o_ref, *, tq, tk, q_off, n_valid, groups):
    lane_q = lax.broadcasted_iota(jnp.int32, (tq, LANES), 1)
    nh = 2 * groups
    half = tq // 2

    def block(qs, qmin, row0, nrows, start, nkeys, carry, masked):
        krb = kr_ref[0, start:start + nkeys, :]
        lane_k = lax.broadcasted_iota(jnp.int32, (nkeys, LANES), 1)
        if masked:
            qpos = qmin + row0 + lax.broadcasted_iota(jnp.int32, (nrows, nkeys), 0)
            klim = jnp.minimum((qpos | (CHUNK - 1)) + 1, n_valid)
            vis = (lax.broadcasted_iota(jnp.int32, (nrows, nkeys), 1) + start) < klim
        new = []
        for hd in range(nh):
            m_run, l_run, acc = carry[3 * hd:3 * hd + 3]
            kvb = kv_ref[0, start:start + nkeys, hd * LANES:(hd + 1) * LANES]
            kf = jnp.where(lane_k < HALF, kvb, krb)
            s = lax.dot_general(qs[hd][row0:row0 + nrows], kf, (((1,), (1,)), ((), ())),
                                preferred_element_type=F32)
            if masked:
                s = jnp.where(vis, s, NEG_BIG)
            m_new = jnp.maximum(m_run, jnp.max(s, axis=1, keepdims=True))
            alpha = jnp.exp2(m_run - m_new)
            p = jnp.exp2(s - m_new)
            l_new = alpha * l_run + jnp.sum(p, axis=1, keepdims=True)
            acc = alpha * acc + jnp.dot(p.astype(BF16), kvb, preferred_element_type=F32)
            new += [m_new, l_new, acc]
        return tuple(new)

    for i in range(q_ref.shape[1] // tq):
        rows = slice(i * tq, (i + 1) * tq)
        qmin = q_off + i * tq
        n_vis = ((qmin + tq - 1) // CHUNK + 1) * CHUNK
        nkb = (min(n_vis, n_valid) + tk - 1) // tk
        n_full = min((qmin | (CHUNK - 1)) + 1, n_valid) // tk
        qs = [q_ref[0, rows, hd * LANES:(hd + 1) * LANES] for hd in range(nh)]
        res = (jnp.full((tq, 1), NEG_BIG, F32), jnp.zeros((tq, 1), F32), jnp.zeros((tq, LANES), F32)) * nh
        for kb in range(n_full):
            res = block(qs, qmin, 0, tq, kb * tk, tk, res, False)
        if (tq == tk and qmin % tk == 0 and half % CHUNK == 0 and half % LANES == 0 and n_valid % tk == 0
                and nkb == n_full + 1):
            top = block(qs, qmin, 0, half, qmin, half, tuple(a[:half] for a in res), True)
            bot = block(qs, qmin, half, half, qmin, tk, tuple(a[half:] for a in res), True)
            res = tuple(jnp.concatenate([a, b], axis=0) for a, b in zip(top, bot))
        else:
            for kb in range(n_full, nkb):
                res = block(qs, qmin, 0, tq, kb * tk, tk, res, True)
        for g in range(groups):
            o0 = res[6 * g + 2] / res[6 * g + 1]
            o1 = res[6 * g + 5] / res[6 * g + 4]
            o = jnp.where(lane_q < HALF, pltpu.roll(o0, HALF, 1), o1)
            o_ref[0, rows, g * LANES:(g + 1) * LANES] = o.astype(o_ref.dtype)


def _mla_attn(q, kv, krp, q_off, n_valid, tq_pref=512):
    bsz, tq_len, _ = q.shape
    groups = _attn_groups(tq_len)
    tk_len = kv.shape[1]
    tq = _tile(tq_len, tq_pref)
    tk = _key_tile(tk_len)
    assert CHUNK & (CHUNK - 1) == 0 and tk_len % tk == 0 and n_valid <= tk_len and C_HEADS % (2 * groups) == 0
    return pl.pallas_call(
        functools.partial(_mla_attn_kernel, tq=tq, tk=tk, q_off=q_off, n_valid=n_valid, groups=groups),
        name="mla_attn",
        grid=(bsz, C_HEADS // (2 * groups)),
        in_specs=[pl.BlockSpec((1, tq_len, 2 * groups * LANES), lambda b, h: (b, 0, h)),
                  pl.BlockSpec((1, tk_len, 2 * groups * LANES), lambda b, h: (b, 0, h)),
                  pl.BlockSpec((1, tk_len, LANES), lambda b, h: (b, 0, 0))],
        out_specs=pl.BlockSpec((1, tq_len, groups * LANES), lambda b, h: (b, 0, h)),
        out_shape=jax.ShapeDtypeStruct((bsz, tq_len, C_HEADS * C_VDIM), BF16),
        compiler_params=_cparams("parallel", "parallel"))(q, kv, krp)


def _pad_rows(a, n):
    return jnp.pad(a, ((0, 0), (0, n - a.shape[1]), (0, 0)))


def _mlstm_layer(x, gain, state, conv0, prev, w, j, bsz, t_len):
    assert t_len >= 8 and t_len % HALO == 0
    conv0p = jnp.pad(conv0, ((0, 0), (8 - (A_CONV - 1), 0), (0, 0)))
    q, k, v, xc, z, gcol, grow, tail = _mlstm_front(x.reshape(bsz, t_len, -1), gain, conv0p, w, j)
    hn, c_st, n_st, m_st = _mlstm_rec(q, k, v, gcol, grow, w["g_head"][j], state, j, prev)
    flat = lambda a: a.reshape(bsz * t_len, A_INNER)
    return (flat(hn), flat(xc), flat(z), w["skip"][j]), (c_st, n_st, m_st), tail[:, 8 - (A_CONV - 1):]


def _sb_layer(x, gain, past, w, j, bsz, t_len, q_off):
    d = x.shape[1]
    q, k, kb, v, vb = _proj(x, gain, [(w["wqkv"], j)],
                            [(0, 0, d, BF16), (0, d, 2 * d, F32), (0, d, 2 * d, BF16),
                             (0, 2 * d, 3 * d, F32), (0, 2 * d, 3 * d, BF16)], name="sb_qkv")
    k3 = kb.reshape(bsz, t_len, d)
    v3 = vb.reshape(bsz, t_len, d)
    if past is not None:
        k3 = jnp.concatenate([past[0].reshape(bsz, -1, d).astype(BF16), k3], axis=1)
        v3 = jnp.concatenate([past[1].reshape(bsz, -1, d).astype(BF16), v3], axis=1)
    tk_pad = -(-k3.shape[1] // LANES) * LANES
    k3 = _pad_rows(k3, tk_pad)
    v3 = _pad_rows(v3, tk_pad)
    o = _sb_attn(q.reshape(bsz, t_len, d), k3, v3, w["u"], q_off)
    return o.reshape(bsz * t_len, d), (k.reshape(bsz, t_len, B_HEADS, B_DH), v.reshape(bsz, t_len, B_HEADS, B_DH))


def _rope_tables(t_len, q_off):
    half = C_ROPE // 2
    inv = ROPE_THETA ** (-jnp.arange(half, dtype=F32) / half)
    ang = (jnp.arange(t_len, dtype=F32) + q_off)[:, None] * inv[None, :]
    cos, sin = jnp.cos(ang), jnp.sin(ang)
    ones = jnp.ones((t_len, HALF), F32)
    zeros = jnp.zeros((t_len, HALF), F32)
    tail = LANES - HALF - C_ROPE
    cos_t = jnp.concatenate([ones, cos, cos, ones[:, :tail]], axis=1)
    sin_s = jnp.concatenate([zeros, -sin, sin, zeros[:, :tail]], axis=1)
    return cos_t, sin_s


def _mla_layer(x, gain, past, w, j, bsz, t_len, q_off):
    cos_t, sin_s = _rope_tables(t_len, q_off)
    outs = _mla_front(x, gain, w, j, cos_t, sin_s, t_len, fuse_kv=past is None)
    q, ckv, krp, krpb = outs[:4]
    ckv3 = ckv.reshape(bsz, t_len, C_KV_RANK)
    kr_new = krp.reshape(bsz, t_len, LANES)[:, :, HALF:HALF + C_ROPE]
    krpb3 = krpb.reshape(bsz, t_len, LANES)
    if past is None:
        assert t_len % LANES == 0
        n_valid = tk_pad = t_len
        kv, krp_all = outs[4], krpb3
    else:
        ckv_all = jnp.concatenate([past[0], ckv3], axis=1)
        kr_pad = jnp.pad(past[1], ((0, 0), (0, 0), (HALF, LANES - HALF - C_ROPE))).astype(BF16)
        n_valid = ckv_all.shape[1]
        tk_pad = -(-n_valid // LANES) * LANES
        krp_all = _pad_rows(jnp.concatenate([kr_pad, krpb3], axis=1), tk_pad)
        (kv,) = _proj(_pad_rows(ckv_all, tk_pad).reshape(bsz * tk_pad, C_KV_RANK), None, [(w["ukv"], j)],
                      [(0, 0, C_HEADS * LANES, BF16)], name="mla_kv_up")
    o = _mla_attn(q.reshape(bsz, t_len, C_HEADS * LANES), kv.reshape(bsz, tk_pad, C_HEADS * LANES),
                  krp_all, q_off, n_valid)
    return o.reshape(bsz * t_len, C_HEADS * C_VDIM), (ckv3, kr_new)


def _trunk(x3, q_off, st_a, past_b, past_c, wts):
    bsz, t_len, d = x3.shape
    x = x3.reshape(bsz * t_len, d)
    a_states, a_convs, new_b, new_c = None, [], [], []
    state = None
    if st_a is not None:
        m_rep = jnp.broadcast_to(st_a[2][..., None], st_a[2].shape + (LANES,))
        state = (st_a[0], st_a[1], m_rep)
    for i in range(DEPTH):
        j = i // 3
        if i % 3 == 0:
            conv0 = jnp.zeros((bsz, A_CONV - 1, A_INNER), F32) if st_a is None else st_a[3][j]
            mixed, a_states, conv_new = _mlstm_layer(x, wts["norm_mix"][i], state, conv0, a_states,
                                                     wts["a"], j, bsz, t_len)
            a_convs.append(conv_new)
            wo = wts["a"]["down"]
        elif i % 3 == 1:
            past = None if past_b is None else (past_b[0][j], past_b[1][j])
            mixed, kv = _sb_layer(x, wts["norm_mix"][i], past, wts["b"], j, bsz, t_len, q_off)
            new_b.append(kv)
            wo = wts["b"]["wo"]
        else:
            past = None if past_c is None else (past_c[0][j], past_c[1][j])
            mixed, lat = _mla_layer(x, wts["norm_mix"][i], past, wts["c"], j, bsz, t_len, q_off)
            new_c.append(lat)
            wo = wts["c"]["wo"]
        final = wts["norm_final"] if i == DEPTH - 1 else None
        x = _mixer_out_mlp(x, mixed, wo, j, wts["norm_mlp"][i], wts["ff1"], wts["ff2"], i, final_gain=final)
    stack = lambda items, idx: jnp.stack([it[idx] for it in items])
    return (x.reshape(bsz, t_len, d),
            a_states[0], a_states[1], a_states[2][..., 0], jnp.stack(a_convs),
            stack(new_b, 0), stack(new_b, 1), stack(new_c, 0), stack(new_c, 1))


def _prep_weights(norm_mix, norm_mlp, norm_final, a_w_up, a_conv_w, a_conv_b, a_w_q, a_w_k, a_w_v,
                  a_w_gate, a_b_i, a_b_f, a_g_head, a_skip, a_w_down, b_w_qkv, b_w_o, c_w_dq, c_g_q,
                  c_w_uq, c_w_dkv, c_g_kv, c_w_ukv, c_w_o, w_ff1, w_ff2):
    bf = lambda a: a.astype(BF16)
    n_a = a_w_up.shape[0]
    wg = a_w_gate.reshape(n_a, 3, A_HEADS, A_DH, 2 * A_HEADS)
    wg = jnp.pad(wg, ((0, 0),) * 4 + ((0, LANES - 2 * A_HEADS),))
    gb = jnp.pad(jnp.concatenate([a_b_i, a_b_f], axis=1), ((0, 0), (0, LANES - 2 * A_HEADS)))
    a = dict(up=bf(a_w_up), conv_w=a_conv_w, conv_b=a_conv_b.reshape(n_a, 1, A_INNER), wq=bf(a_w_q),
             wk=bf(a_w_k), wv=bf(a_w_v), wg=bf(wg), gb=gb.reshape(n_a, 1, LANES),
             g_head=a_g_head.reshape(n_a, 1, A_INNER), skip=a_skip.reshape(n_a, 1, A_INNER), down=bf(a_w_down))
    d = b_w_o.shape[1]
    qscale = jnp.where(jnp.arange(3 * d) < d, (B_DH ** -0.5) * LOG2E, 1.0).astype(F32)
    jj = lax.broadcasted_iota(jnp.int32, (2 * LANES, 2 * LANES), 0) % LANES
    ss = lax.broadcasted_iota(jnp.int32, (2 * LANES, 2 * LANES), 1)
    b = dict(wqkv=bf(b_w_qkv * qscale), wo=bf(b_w_o), u=-((ss >= LANES) | (jj > ss)).astype(BF16))
    n_c = c_w_dq.shape[0]
    uq = c_w_uq.reshape(n_c, -1, C_HEADS, C_NOPE + C_ROPE)
    uq = jnp.pad(uq, ((0, 0),) * 3 + ((0, LANES - C_NOPE - C_ROPE),)).reshape(n_c, -1, C_HEADS * LANES)
    dkv = jnp.concatenate([c_w_dkv[..., :C_KV_RANK], jnp.zeros(c_w_dkv.shape[:2] + (HALF,), F32),
                           c_w_dkv[..., C_KV_RANK:],
                           jnp.zeros(c_w_dkv.shape[:2] + (LANES - HALF - C_ROPE,), F32)], axis=-1)
    c = dict(dq=bf(c_w_dq), dkv=bf(dkv), g_q=c_g_q.reshape(n_c, 1, -1), g_kv=c_g_kv.reshape(n_c, 1, -1),
             uq=bf(uq), ukv=bf(c_w_ukv.reshape(n_c, C_KV_RANK, C_HEADS * LANES)), wo=bf(c_w_o))
    return dict(norm_mix=norm_mix, norm_mlp=norm_mlp, norm_final=norm_final, a=a, b=b, c=c,
                ff1=bf(w_ff1), ff2=bf(w_ff2))


def kernel(x_prompt, x_sample, state_mlstm_C, state_mlstm_n, state_mlstm_m, state_mlstm_conv, cache_sb_k, cache_sb_v, cache_mla_ckv, cache_mla_krope, norm_mix, norm_mlp, norm_final, a_w_up, a_conv_w, a_conv_b, a_w_q, a_w_k, a_w_v, a_w_gate, a_b_i, a_b_f, a_g_head, a_skip, a_w_down, b_w_qkv, b_w_o, c_w_dq, c_g_q, c_w_uq, c_w_dkv, c_g_kv, c_w_ukv, c_w_o, w_ff1, w_ff2):
    wts = _prep_weights(norm_mix, norm_mlp, norm_final, a_w_up, a_conv_w, a_conv_b, a_w_q, a_w_k, a_w_v,
                        a_w_gate, a_b_i, a_b_f, a_g_head, a_skip, a_w_down, b_w_qkv, b_w_o, c_w_dq, c_g_q,
                        c_w_uq, c_w_dkv, c_g_kv, c_w_ukv, c_w_o, w_ff1, w_ff2)
    outs_p = _trunk(x_prompt, 0, None, None, None, wts)
    past_len = cache_sb_k.shape[2]
    outs_s = _trunk(x_sample, past_len,
                    (state_mlstm_C, state_mlstm_n, state_mlstm_m, state_mlstm_conv),
                    (cache_sb_k, cache_sb_v), (cache_mla_ckv, cache_mla_krope), wts)
    y_p, rest_p = outs_p[0], outs_p[1:]
    y_s, rest_s = outs_s[0], outs_s[1:]
    return (y_p, y_s) + tuple(rest_p) + tuple(rest_s)
```

```python
import functools

import jax
import jax.numpy as jnp
from jax import lax
from jax.experimental import pallas as pl
from jax.experimental.pallas import tpu as pltpu

F32 = jnp.float32
BF16 = jnp.bfloat16

EPS = 1e-6
DEPTH = 4
CHUNK = 64
A_HEADS = 4
A_DH = 512
A_INNER = A_HEADS * A_DH
A_CONV = 4
B_HEADS = 16
B_DH = 64
C_HEADS = 16
C_NOPE = 64
C_ROPE = 32
C_VDIM = 64
C_KV_RANK = 256
C_SCALE = (C_NOPE + C_ROPE) ** -0.5
ROPE_THETA = 10000.0

LANES = 128
MXU_DIM = 256
HALF = LANES // 2
NEG_BIG = -1e30
LOG2E = 1.4426950408889634
VMEM_LIMIT_BYTES = 56 * 1024 * 1024


def _cparams(*sem):
    return pltpu.CompilerParams(dimension_semantics=sem, vmem_limit_bytes=VMEM_LIMIT_BYTES)


def _tile(n, pref):
    if n <= pref:
        return n
    t = pref
    while n % t:
        t //= 2
    return t


def _const_spec(shape, layer=None):
    nd = len(shape)
    if layer is None:
        return pl.BlockSpec(shape, lambda *_: (0,) * nd)
    return pl.BlockSpec((None,) + tuple(shape[1:]), lambda *_: (layer,) + (0,) * (nd - 1))


def _rms(x, g):
    return x * lax.rsqrt(jnp.mean(x * x, axis=-1, keepdims=True) + EPS) * g


def _log_sigmoid(x):
    return jnp.minimum(x, 0.0) - jnp.log1p(jnp.exp(-jnp.abs(x)))


def _proj_kernel(*refs, n_w, outs, has_norm, has_res):
    refs = list(refs)
    x_ref = refs.pop(0)
    g_ref = refs.pop(0) if has_norm else None
    w_refs = [refs.pop(0) for _ in range(n_w)]
    r_ref = refs.pop(0) if has_res else None
    if has_norm:
        h = _rms(x_ref[...].astype(F32), g_ref[...]).astype(BF16)
    else:
        h = x_ref[...].astype(BF16)
    groups = {}
    for o_ref, (wi, c0, c1, _) in zip(refs, outs):
        groups.setdefault((wi, c0, c1), []).append(o_ref)
    for gi, ((wi, c0, c1), dests) in enumerate(groups.items()):
        n = c1 - c0
        c = n if n <= 512 else 512
        for n0 in range(0, n, c):
            y = jnp.dot(h, w_refs[wi][:, c0 + n0:c0 + n0 + c], preferred_element_type=F32)
            if has_res and gi == 0:
                y = y + r_ref[:, n0:n0 + c]
            for o_ref in dests:
                o_ref[:, n0:n0 + c] = y.astype(o_ref.dtype)


def _proj(x, gain, weights, outs, residual=None, tm_pref=512, name="proj"):
    m, k = x.shape
    tm = _tile(m, tm_pref)
    in_specs = [pl.BlockSpec((tm, k), lambda i: (i, 0))]
    args = [x]
    if gain is not None:
        in_specs.append(_const_spec((1, k)))
        args.append(gain.reshape(1, k).astype(F32))
    for w, layer in weights:
        in_specs.append(_const_spec(w.shape, layer))
        args.append(w)
    if residual is not None:
        in_specs.append(pl.BlockSpec((tm, residual.shape[1]), lambda i: (i, 0)))
        args.append(residual)
    out_shape = [jax.ShapeDtypeStruct((m, c1 - c0), dt) for _, c0, c1, dt in outs]
    out_specs = [pl.BlockSpec((tm, c1 - c0), lambda i: (i, 0)) for _, c0, c1, _ in outs]
    kern = functools.partial(_proj_kernel, n_w=len(weights), outs=tuple(outs),
                             has_norm=gain is not None, has_res=residual is not None)
    return pl.pallas_call(kern, name=name, grid=(m // tm,), in_specs=in_specs, out_specs=out_specs,
                          out_shape=out_shape, compiler_params=_cparams("parallel"))(*args)


def _mlp_kernel(*refs, pre, final, tf):
    refs = list(refs)
    x_ref = refs.pop(0)
    if pre == "mlstm":
        hn_ref, xc_ref, z_ref, skip_ref = (refs.pop(0) for _ in range(4))
        z = z_ref[...].astype(F32)
        a = (hn_ref[...].astype(F32) + skip_ref[...] * xc_ref[...].astype(F32)) * (1.0 / (1.0 + jnp.exp(-z)))
        a = a.astype(BF16)
    else:
        a = refs.pop(0)[...]
    wo_ref, g_ref, w1_ref, w2_ref = (refs.pop(0) for _ in range(4))
    gf_ref = refs.pop(0) if final else None
    (o_ref,) = refs
    x = x_ref[...] + jnp.dot(a, wo_ref[...], preferred_element_type=F32)
    h = _rms(x, g_ref[...]).astype(BF16)
    acc = x
    for f0 in range(0, w1_ref.shape[1], tf):
        u = jnp.dot(h, w1_ref[:, f0:f0 + tf], preferred_element_type=F32)
        u = jnp.maximum(u, 0.0)
        u = u * u
        acc = acc + jnp.dot(u.astype(BF16), w2_ref[f0:f0 + tf, :], preferred_element_type=F32)
    if final:
        acc = _rms(acc, gf_ref[...])
    o_ref[...] = acc


def _mixer_out_mlp(x, mixed, wo, wo_layer, gain, w1, w2, layer, final_gain=None, tm_pref=512, tf_pref=1024):
    m, d = x.shape
    dff = w1.shape[2]
    tm = _tile(m, tm_pref)
    tf = _tile(dff, tf_pref)
    rows = lambda n: pl.BlockSpec((tm, n), lambda i: (i, 0))
    resident = lambda arr, lyr: pl.BlockSpec((None,) + arr.shape[1:], lambda i: (lyr, 0, 0),
                                             pipeline_mode=pl.Buffered(1))
    in_specs, args = [rows(d)], [x]
    if isinstance(mixed, tuple):
        hn, xc, z, skip = mixed
        pre = "mlstm"
        in_specs += [rows(hn.shape[1]), rows(xc.shape[1]), rows(z.shape[1]), _const_spec(skip.shape)]
        args += [hn, xc, z, skip]
    else:
        pre = "proj"
        in_specs.append(rows(mixed.shape[1]))
        args.append(mixed)
    in_specs += [resident(wo, wo_layer), _const_spec((1, d)), resident(w1, layer), resident(w2, layer)]
    args += [wo, gain.reshape(1, d), w1, w2]
    if final_gain is not None:
        in_specs.append(_const_spec((1, d)))
        args.append(final_gain.reshape(1, d))
    return pl.pallas_call(
        functools.partial(_mlp_kernel, pre=pre, final=final_gain is not None, tf=tf), name="out_mlp",
        grid=(m // tm,), in_specs=in_specs, out_specs=rows(d),
        out_shape=jax.ShapeDtypeStruct((m, d), F32),
        compiler_params=_cparams("parallel"))(*args)


HALO = 16


def _mlstm_front_kernel(x_ref, xh_ref, conv0_ref, gn_ref, wup_ref, cw_ref, cb_ref, wq_ref, wk_ref, wv_ref,
                        wg_ref, gb_ref, q_ref, k_ref, v_ref, xc_ref, z_ref, gcol_ref, grow_ref, tail_ref,
                        xp_ref, *, tt, nb):
    t = pl.program_id(1)
    d = x_ref.shape[2]
    hn = _rms(x_ref[...].reshape(nb * tt, d), gn_ref[...]).astype(BF16)
    hh = _rms(xh_ref[...].reshape(nb * HALO, d), gn_ref[...]).astype(BF16)
    xms = []
    for h in range(A_HEADS):
        sl = slice(h * A_DH, (h + 1) * A_DH)
        xm = jnp.dot(hn, wup_ref[:, sl], preferred_element_type=F32)
        xm_prev = jnp.dot(hh, wup_ref[:, sl], preferred_element_type=F32)
        for b in range(nb):
            prev = xm_prev[b * HALO + HALO - 8:(b + 1) * HALO]
            xp_ref[b, 0:8, sl] = jnp.where(t == 0, conv0_ref[b, :, sl], prev)
            xp_ref[b, 8:8 + tt, sl] = xm[b * tt:(b + 1) * tt]
        xms.append(xm)
    for h in range(A_HEADS):
        sl = slice(h * A_DH, (h + 1) * A_DH)
        zh = jnp.dot(hn, wup_ref[:, A_INNER + h * A_DH:A_INNER + (h + 1) * A_DH], preferred_element_type=F32)
        z_ref[:, :, sl] = zh.astype(BF16).reshape(nb, tt, A_DH)
    g = jnp.zeros((nb * tt, LANES), F32)
    for h in range(A_HEADS):
        sl = slice(h * A_DH, (h + 1) * A_DH)
        xm = xms[h]
        parts = []
        for b in range(nb):
            xc = cb_ref[:, sl] + xm[b * tt:(b + 1) * tt] * cw_ref[A_CONV - 1:A_CONV, sl]
            for j in range(A_CONV - 1):
                xc = xc + xp_ref[b, 5 + j:5 + j + tt, sl] * cw_ref[j:j + 1, sl]
            parts.append(xc / (1.0 + jnp.exp(-xc)))
        xch = (parts[0] if nb == 1 else jnp.concatenate(parts, axis=0)).astype(BF16)
        xc_ref[:, :, sl] = xch.reshape(nb, tt, A_DH)
        qh = jnp.dot(xch, wq_ref[h], preferred_element_type=F32).astype(BF16)
        kh = (jnp.dot(xch, wk_ref[h], preferred_element_type=F32) * (A_DH ** -0.5)).astype(BF16)
        vh = jnp.dot(xm.astype(BF16), wv_ref[h], preferred_element_type=F32).astype(BF16)
        q_ref[:, :, sl] = qh.reshape(nb, tt, A_DH)
        k_ref[:, :, sl] = kh.reshape(nb, tt, A_DH)
        v_ref[:, :, sl] = vh.reshape(nb, tt, A_DH)
        g = g + jnp.dot(qh, wg_ref[0, h], preferred_element_type=F32)
        g = g + jnp.dot(kh, wg_ref[1, h], preferred_element_type=F32)
        g = g + jnp.dot(vh, wg_ref[2, h], preferred_element_type=F32)
    for b in range(nb):
        tail_ref[b] = xp_ref[b, tt:tt + 8, :]
    g = g + gb_ref[...]
    lane = lax.broadcasted_iota(jnp.int32, (nb * tt, LANES), 1)
    g = jnp.where(lane < A_HEADS, g, _log_sigmoid(g))
    gcol_ref[...] = g.reshape(nb, tt, LANES)
    sel = (lax.broadcasted_iota(jnp.int32, (8, LANES), 0) ==
           lax.broadcasted_iota(jnp.int32, (8, LANES), 1)).astype(BF16)
    for b in range(nb):
        grow = jnp.zeros((8, tt), F32)
        rem = g[b * tt:(b + 1) * tt]
        for _ in range(3):
            part = rem.astype(BF16)
            grow = grow + lax.dot_general(sel, part, (((1,), (1,)), ((), ())), preferred_element_type=F32)
            rem = rem - part.astype(F32)
        grow_ref[b] = grow


def _mlstm_front(x3, gain, conv0p, w, layer, tt_pref=512):
    bsz, t_len, d = x3.shape
    tt = _tile(t_len, tt_pref)
    nb = _tile(bsz, max(1, tt_pref // tt))
    nhalo = tt // HALO
    resident = lambda arr: pl.BlockSpec((None,) + arr.shape[1:], lambda b, t, nd=arr.ndim: (layer,) + (0,) * (nd - 1),
                                        pipeline_mode=pl.Buffered(1))
    act = lambda dt: jax.ShapeDtypeStruct((bsz, t_len, A_INNER), dt)
    act_spec = pl.BlockSpec((nb, tt, A_INNER), lambda b, t: (b, t, 0))
    cw, cb, gb = w["conv_w"][layer], w["conv_b"][layer], w["gb"][layer]
    return pl.pallas_call(
        functools.partial(_mlstm_front_kernel, tt=tt, nb=nb), name="mlstm_front",
        grid=(bsz // nb, t_len // tt),
        in_specs=[pl.BlockSpec((nb, tt, d), lambda b, t: (b, t, 0)),
                  pl.BlockSpec((nb, HALO, d), lambda b, t: (b, jnp.maximum(t * nhalo - 1, 0), 0)),
                  pl.BlockSpec((nb, 8, A_INNER), lambda b, t: (b, 0, 0)),
                  _const_spec((1, d)), resident(w["up"]),
                  _const_spec(cw.shape), _const_spec(cb.shape), resident(w["wq"]), resident(w["wk"]),
                  resident(w["wv"]), resident(w["wg"]), _const_spec(gb.shape)],
        out_specs=[act_spec, act_spec, act_spec, act_spec, act_spec,
                   pl.BlockSpec((nb, tt, LANES), lambda b, t: (b, t, 0)),
                   pl.BlockSpec((nb, 8, tt), lambda b, t: (b, 0, t)),
                   pl.BlockSpec((nb, 8, A_INNER), lambda b, t: (b, 0, 0))],
        out_shape=[act(BF16), act(BF16), act(BF16), act(BF16), act(BF16),
                   jax.ShapeDtypeStruct((bsz, t_len, LANES), F32),
                   jax.ShapeDtypeStruct((bsz, 8, t_len), F32),
                   jax.ShapeDtypeStruct((bsz, 8, A_INNER), F32)],
        scratch_shapes=[pltpu.VMEM((nb, tt + 8, A_INNER), F32)],
        compiler_params=_cparams("parallel", "arbitrary"))(
            x3, x3, conv0p, gain.reshape(1, d), w["up"], cw, cb, w["wq"], w["wk"], w["wv"], w["wg"], gb)


def _mlstm_rec_kernel(*refs, lc, has_state, n_prev, n_chunks):
    refs = list(refs)
    q_ref, k_ref, v_ref, gcol_ref, grow_ref, gh_ref = (refs.pop(0) for _ in range(6))
    init_refs = [refs.pop(0) for _ in range(3)] if has_state else None
    prev_refs = [refs.pop(0) for _ in range(3)] if n_prev else None
    hn_ref, cs_ref, ns_ref, ms_ref = refs
    c_ref, n_ref, m_ref = cs_ref.at[n_prev], ns_ref.at[n_prev], ms_ref.at[n_prev]
    c_idx = pl.program_id(1)

    @pl.when(c_idx == 0)
    def _():
        for st_ref, i_ref in zip((c_ref, n_ref, m_ref), init_refs or (None,) * 3):
            st_ref[...] = jnp.zeros_like(st_ref) if i_ref is None else i_ref[...]
        if n_prev:
            for out_ref, p_ref in zip((cs_ref, ns_ref, ms_ref), prev_refs):
                out_ref[0:n_prev] = p_ref[...]

    row = lax.broadcasted_iota(jnp.int32, (lc, lc), 0)
    col = lax.broadcasted_iota(jnp.int32, (lc, lc), 1)
    causal = col <= row
    for ci, h in [(ci, h) for ci in range(n_chunks) for h in range(A_HEADS)]:
        rows = slice(ci * lc, (ci + 1) * lc)
        sl = slice(h * A_DH, (h + 1) * A_DH)
        qh = q_ref[0, rows, sl]
        kh = k_ref[0, rows, sl]
        vh = v_ref[0, rows, sl]
        li_c = gcol_ref[0, rows, h:h + 1]
        lf_c = gcol_ref[0, rows, A_HEADS + h:A_HEADS + h + 1]
        li_r = grow_ref[0, h:h + 1, rows]
        lf_r = grow_ref[0, A_HEADS + h:A_HEADS + h + 1, rows]
        b_c = jnp.sum(jnp.where(causal, lf_r, 0.0), axis=1, keepdims=True)
        b_r = jnp.sum(jnp.where(row <= col, lf_c, 0.0), axis=0, keepdims=True)
        c_old = c_ref[0, h]
        n_old = n_ref[0, h:h + 1, :]
        m_old = m_ref[0, h:h + 1, 0:1]
        dmat = jnp.where(causal, b_c - b_r + li_r, NEG_BIG)
        inter = b_c + m_old
        m_t = jnp.maximum(inter, jnp.max(dmat, axis=1, keepdims=True))
        w_inter = jnp.exp(inter - m_t)
        s = lax.dot_general(qh, kh, (((1,), (1,)), ((), ())), preferred_element_type=F32)
        s = s * jnp.exp(dmat - m_t)
        num = w_inter * jnp.dot(qh, c_old.astype(BF16), preferred_element_type=F32)
        num = num + jnp.dot(s.astype(BF16), vh, preferred_element_type=F32)
        nq = w_inter * jnp.sum(qh.astype(F32) * n_old, axis=1, keepdims=True)
        nq = nq + jnp.sum(s, axis=1, keepdims=True)
        hv = num / jnp.maximum(jnp.abs(nq), jnp.exp(-m_t))
        mu = jnp.mean(hv, axis=1, keepdims=True)
        hc = hv - mu
        var = jnp.mean(hc * hc, axis=1, keepdims=True)
        hn_ref[0, rows, sl] = (hc * lax.rsqrt(var + EPS) * gh_ref[:, sl]).astype(hn_ref.dtype)
        m_new = m_t[lc - 1:lc, :]
        b_last = b_c[lc - 1:lc, :]
        g_state = jnp.exp(b_last + m_old - m_new)
        g_tok = jnp.exp(b_last - b_c + li_c - m_new)
        kw = kh.astype(F32) * g_tok
        c_ref[0, h] = g_state * c_old + jnp.dot(kw.T.astype(BF16), vh, preferred_element_type=F32)
        n_ref[0, h:h + 1, :] = g_state * n_old + jnp.sum(kw, axis=0, keepdims=True)
        m_ref[0, h:h + 1, :] = jnp.broadcast_to(m_new, (1, LANES))


def _mlstm_rec(q, k, v, gcol, grow, g_head, state, layer, prev, lc_pref=256, chunks_pref=2):
    bsz, t_len, _ = q.shape
    lc = _tile(t_len, lc_pref)
    n_chunks = _tile(t_len // lc, chunks_pref)
    ts = lc * n_chunks
    n_prev = 0 if prev is None else prev[0].shape[0]
    act_spec = pl.BlockSpec((1, ts, A_INNER), lambda b, c: (b, c, 0))
    st_shapes = [(1, A_HEADS, A_DH, A_DH), (1, A_HEADS, A_DH), (1, A_HEADS, LANES)]
    stacked = lambda n: [pl.BlockSpec((n,) + shp, lambda b, c, nd=len(shp): (0, b) + (0,) * (nd - 1))
                         for shp in st_shapes]
    in_specs = [act_spec, act_spec, act_spec,
                pl.BlockSpec((1, ts, LANES), lambda b, c: (b, c, 0)),
                pl.BlockSpec((1, 8, ts), lambda b, c: (b, 0, c)),
                _const_spec((1, A_INNER))]
    args = [q, k, v, gcol, grow, g_head]
    if state is not None:
        in_specs += [pl.BlockSpec((None,) + shp, lambda b, c, nd=len(shp): (layer, b) + (0,) * (nd - 1))
                     for shp in st_shapes]
        args += list(state)
    if n_prev:
        in_specs += stacked(n_prev)
        args += list(prev)
    return pl.pallas_call(
        functools.partial(_mlstm_rec_kernel, lc=lc, has_state=state is not None, n_prev=n_prev,
                          n_chunks=n_chunks),
        name="mlstm_rec", grid=(bsz, t_len // ts), in_specs=in_specs,
        out_specs=[act_spec] + stacked(n_prev + 1),
        out_shape=[jax.ShapeDtypeStruct((bsz, t_len, A_INNER), BF16)] +
                  [jax.ShapeDtypeStruct((n_prev + 1, bsz) + shp[1:], F32) for shp in st_shapes],
        compiler_params=_cparams("parallel", "arbitrary"))(*args)


def _sb_attn_kernel(q_ref, k_ref, v_ref, u_ref, o_ref, *, tq, tk, q_off, groups):
    lane = lax.broadcasted_iota(jnp.int32, (tq, LANES), 1)
    u = u_ref[...]
    n_carry = 4 * groups
    half = tq // 2

    def block(qms, qmin, row0, nrows, start, nkeys, carry, masked):
        nsub = nkeys // LANES
        if masked:
            qpos = qmin + row0 + lax.broadcasted_iota(jnp.int32, (nrows, nkeys), 0)
            before = (lax.broadcasted_iota(jnp.int32, (nrows, nkeys), 1) + start) < qpos
        new = []
        for hd in range(2 * groups):
            sl = slice((hd // 2) * LANES, (hd // 2 + 1) * LANES)
            k2 = k_ref[0, start:start + nkeys, sl]
            v2 = v_ref[0, start:start + nkeys, sl]
            acc, run = carry[2 * hd], carry[2 * hd + 1]
            z = lax.dot_general(qms[hd][row0:row0 + nrows], k2, (((1,), (1,)), ((), ())),
                                preferred_element_type=F32)
            sp = jnp.maximum(z, 0.0) + jnp.log(1.0 + jnp.exp2(-jnp.abs(z))) * LOG2E
            nlf = jnp.where(before, sp, 0.0) if masked else sp
            hi = nlf.astype(BF16)
            lo = (nlf - hi.astype(F32)).astype(BF16)
            pieces = [jnp.concatenate([hi[:, sb * LANES:(sb + 1) * LANES], lo[:, sb * LANES:(sb + 1) * LANES]], axis=1)
                      for sb in range(nsub)]
            if nsub > 1 and nrows < MXU_DIM:
                r_all = jnp.dot(jnp.concatenate(pieces, axis=0), u, preferred_element_type=F32)
                rs = [r_all[sb * nrows:(sb + 1) * nrows] for sb in range(nsub)]
            else:
                rs = [jnp.dot(p, u, preferred_element_type=F32) for p in pieces]
            rests = [None] * nsub
            for sb in reversed(range(nsub)):
                rests[sb] = rs[sb][:, :LANES] + run
                run = run + rs[sb][:, LANES:]
            rest = rests[0] if nsub == 1 else jnp.concatenate(rests, axis=1)
            att = jnp.exp2(z - sp + rest)
            if masked:
                att = jnp.where(before, att, 0.0)
            acc = acc + jnp.dot(att.astype(BF16), v2, preferred_element_type=F32)
            new += [acc, run]
        return tuple(new)

    for i in range(q_ref.shape[1] // tq):
        rows = slice(i * tq, (i + 1) * tq)
        qmin = q_off + i * tq
        nkb = (qmin + tq - 2) // tk + 1
        n_diag = nkb - qmin // tk
        qms = []
        for g in range(groups):
            q2 = q_ref[0, rows, g * LANES:(g + 1) * LANES]
            qms += [jnp.where((lane >= HALF * e) & (lane < HALF * (e + 1)), q2, jnp.zeros_like(q2))
                    for e in range(2)]
        if tq == tk and qmin % tk == 0 and half % LANES == 0:
            zeros = jnp.zeros((half, LANES), F32)
            top = block(qms, qmin, 0, half, qmin, half, (zeros,) * n_carry, True)
            bot = block(qms, qmin, half, half, qmin, tk, (zeros,) * n_carry, True)
            res = tuple(jnp.concatenate([a, b], axis=0) for a, b in zip(top, bot))
            n_diag = 1
        else:
            res = (jnp.zeros((tq, LANES), F32),) * n_carry
            for j in range(n_diag):
                res = block(qms, qmin, 0, tq, (nkb - 1 - j) * tk, tk, res, True)
        for j in range(n_diag, nkb):
            res = block(qms, qmin, 0, tq, (nkb - 1 - j) * tk, tk, res, False)
        for g in range(groups):
            o_ref[0, rows, g * LANES:(g + 1) * LANES] = jnp.where(
                lane < HALF, res[4 * g], res[4 * g + 2]).astype(o_ref.dtype)


def _key_tile(tk_len):
    return tk_len if tk_len <= 10 * LANES else 4 * LANES


def _attn_groups(tq_len):
    return 4 if tq_len < LANES else 1


def _sb_attn(q, k_all, v_all, u, q_off, tq_pref=512):
    bsz, tq_len, d = q.shape
    groups = _attn_groups(tq_len)
    tk_len = k_all.shape[1]
    tq = _tile(tq_len, tq_pref)
    tk = _key_tile(tk_len)
    gw = groups * LANES
    assert tk_len % tk == 0 and (q_off + tq_len - 2) // tk < tk_len // tk and d % gw == 0
    return pl.pallas_call(
        functools.partial(_sb_attn_kernel, tq=tq, tk=tk, q_off=q_off, groups=groups), name="sb_attn",
        grid=(bsz, d // gw),
        in_specs=[pl.BlockSpec((1, tq_len, gw), lambda b, h: (b, 0, h)),
                  pl.BlockSpec((1, tk_len, gw), lambda b, h: (b, 0, h)),
                  pl.BlockSpec((1, tk_len, gw), lambda b, h: (b, 0, h)),
                  _const_spec(u.shape)],
        out_specs=pl.BlockSpec((1, tq_len, gw), lambda b, h: (b, 0, h)),
        out_shape=jax.ShapeDtypeStruct((bsz, tq_len, d), BF16),
        compiler_params=_cparams("parallel", "parallel"))(q, k_all, v_all, u)


def _rope128(x, cos_t, sin_s, lane):
    rot = jnp.where(lane < HALF + C_ROPE // 2, pltpu.roll(x, LANES - C_ROPE // 2, 1),
                    pltpu.roll(x, C_ROPE // 2, 1))
    return x * cos_t + rot * sin_s


def _mla_front_kernel(x_ref, g_ref, wdq_ref, wdkv_ref, gq_ref, gkv_ref, wuq_ref, cos_ref, sin_ref, *rest,
                      fuse_kv):
    if fuse_kv:
        wukv_ref, q_ref, ckvn_ref, krp_ref, krpb_ref, kv_ref = rest
    else:
        q_ref, ckvn_ref, krp_ref, krpb_ref = rest
    tm = x_ref.shape[0]
    lane = lax.broadcasted_iota(jnp.int32, (tm, LANES), 1)
    cos_t = cos_ref[...]
    sin_s = sin_ref[...]
    h = _rms(x_ref[...], g_ref[...]).astype(BF16)
    cq = _rms(jnp.dot(h, wdq_ref[...], preferred_element_type=F32), gq_ref[...]).astype(BF16)
    kva = jnp.dot(h, wdkv_ref[...], preferred_element_type=F32)
    for hd in range(C_HEADS):
        sl = slice(hd * LANES, (hd + 1) * LANES)
        qh = jnp.dot(cq, wuq_ref[:, sl], preferred_element_type=F32)
        q_ref[:, sl] = (_rope128(qh, cos_t, sin_s, lane) * (C_SCALE * LOG2E)).astype(BF16)
    ckv = _rms(kva[:, :C_KV_RANK], gkv_ref[...])
    ckvn_ref[...] = ckv
    krp = _rope128(kva[:, C_KV_RANK:], cos_t, sin_s, lane)
    krp_ref[...] = krp
    krpb_ref[...] = krp.astype(BF16)
    if fuse_kv:
        ckv_b = ckv.astype(BF16)
        for n0 in range(0, kv_ref.shape[1], 512):
            kv_ref[:, n0:n0 + 512] = jnp.dot(ckv_b, wukv_ref[:, n0:n0 + 512],
                                             preferred_element_type=F32).astype(BF16)


def _mla_front(x, gain, w, layer, cos_t, sin_s, t_len, fuse_kv, tm_pref=512):
    m, d = x.shape
    tm = _tile(t_len, tm_pref)
    nt = t_len // tm
    rows = lambda n: pl.BlockSpec((tm, n), lambda i: (i, 0))
    tab = pl.BlockSpec((tm, LANES), lambda i: (i % nt, 0))
    in_specs = [rows(d), _const_spec((1, d)), _const_spec(w["dq"].shape, layer), _const_spec(w["dkv"].shape, layer),
                _const_spec(w["g_q"].shape[1:]), _const_spec(w["g_kv"].shape[1:]), _const_spec(w["uq"].shape, layer),
                tab, tab]
    args = [x, gain.reshape(1, d), w["dq"], w["dkv"], w["g_q"][layer], w["g_kv"][layer], w["uq"], cos_t, sin_s]
    out_specs = [rows(C_HEADS * LANES), rows(C_KV_RANK), rows(LANES), rows(LANES)]
    out_shape = [jax.ShapeDtypeStruct((m, C_HEADS * LANES), BF16), jax.ShapeDtypeStruct((m, C_KV_RANK), F32),
                 jax.ShapeDtypeStruct((m, LANES), F32), jax.ShapeDtypeStruct((m, LANES), BF16)]
    if fuse_kv:
        in_specs.append(_const_spec(w["ukv"].shape, layer))
        args.append(w["ukv"])
        out_specs.append(rows(C_HEADS * LANES))
        out_shape.append(jax.ShapeDtypeStruct((m, C_HEADS * LANES), BF16))
    return pl.pallas_call(
        functools.partial(_mla_front_kernel, fuse_kv=fuse_kv), name="mla_front", grid=(m // tm,),
        in_specs=in_specs, out_specs=out_specs, out_shape=out_shape,
        compiler_params=_cparams("parallel"))(*args)


def _mla_attn_kernel(q_ref, kv_ref, kr_ref, o_ref, *, tq, tk, q_off, n_valid, groups):
    lane_q = lax.broadcasted_iota(jnp.int32, (tq, LANES), 1)
    nh = 2 * groups
    half = tq // 2

    def block(qs, qmin, row0, nrows, start, nkeys, carry, masked):
        krb = kr_ref[0, start:start + nkeys, :]
        lane_k = lax.broadcasted_iota(jnp.int32, (nkeys, LANES), 1)
        if masked:
            qpos = qmin + row0 + lax.broadcasted_iota(jnp.int32, (nrows, nkeys), 0)
            klim = jnp.minimum((qpos | (CHUNK - 1)) + 1, n_valid)
            vis = (lax.broadcasted_iota(jnp.int32, (nrows, nkeys), 1) + start) < klim
        new = []
        for hd in range(nh):
            m_run, l_run, acc = carry[3 * hd:3 * hd + 3]
            kvb = kv_ref[0, start:start + nkeys, hd * LANES:(hd + 1) * LANES]
            kf = jnp.where(lane_k < HALF, kvb, krb)
            s = lax.dot_general(qs[hd][row0:row0 + nrows], kf, (((1,), (1,)), ((), ())),
                                preferred_element_type=F32)
            if masked:
                s = jnp.where(vis, s, NEG_BIG)
            m_new = jnp.maximum(m_run, jnp.max(s, axis=1, keepdims=True))
            alpha = jnp.exp2(m_run - m_new)
            p = jnp.exp2(s - m_new)
            l_new = alpha * l_run + jnp.sum(p, axis=1, keepdims=True)
            acc = alpha * acc + jnp.dot(p.astype(BF16), kvb, preferred_element_type=F32)
            new += [m_new, l_new, acc]
        return tuple(new)

    for i in range(q_ref.shape[1] // tq):
        rows = slice(i * tq, (i + 1) * tq)
        qmin = q_off + i * tq
        n_vis = ((qmin + tq - 1) // CHUNK + 1) * CHUNK
        nkb = (min(n_vis, n_valid) + tk - 1) // tk
        n_full = min((qmin | (CHUNK - 1)) + 1, n_valid) // tk
        qs = [q_ref[0, rows, hd * LANES:(hd + 1) * LANES] for hd in range(nh)]
        res = (jnp.full((tq, 1), NEG_BIG, F32), jnp.zeros((tq, 1), F32), jnp.zeros((tq, LANES), F32)) * nh
        for kb in range(n_full):
            res = block(qs, qmin, 0, tq, kb * tk, tk, res, False)
        if (tq == tk and qmin % tk == 0 and half % CHUNK == 0 and half % LANES == 0 and n_valid % tk == 0
                and nkb == n_full + 1):
            top = block(qs, qmin, 0, half, qmin, half, tuple(a[:half] for a in res), True)
            bot = block(qs, qmin, half, half, qmin, tk, tuple(a[half:] for a in res), True)
            res = tuple(jnp.concatenate([a, b], axis=0) for a, b in zip(top, bot))
        else:
            for kb in range(n_full, nkb):
                res = block(qs, qmin, 0, tq, kb * tk, tk, res, True)
        for g in range(groups):
            o0 = res[6 * g + 2] / res[6 * g + 1]
            o1 = res[6 * g + 5] / res[6 * g + 4]
            o = jnp.where(lane_q < HALF, pltpu.roll(o0, HALF, 1), o1)
            o_ref[0, rows, g * LANES:(g + 1) * LANES] = o.astype(o_ref.dtype)


def _mla_attn(q, kv, krp, q_off, n_valid, tq_pref=512):
    bsz, tq_len, _ = q.shape
    groups = _attn_groups(tq_len)
    tk_len = kv.shape[1]
    tq = _tile(tq_len, tq_pref)
    tk = _key_tile(tk_len)
    assert CHUNK & (CHUNK - 1) == 0 and tk_len % tk == 0 and n_valid <= tk_len and C_HEADS % (2 * groups) == 0
    return pl.pallas_call(
        functools.partial(_mla_attn_kernel, tq=tq, tk=tk, q_off=q_off, n_valid=n_valid, groups=groups),
        name="mla_attn",
        grid=(bsz, C_HEADS // (2 * groups)),
        in_specs=[pl.BlockSpec((1, tq_len, 2 * groups * LANES), lambda b, h: (b, 0, h)),
                  pl.BlockSpec((1, tk_len, 2 * groups * LANES), lambda b, h: (b, 0, h)),
                  pl.BlockSpec((1, tk_len, LANES), lambda b, h: (b, 0, 0))],
        out_specs=pl.BlockSpec((1, tq_len, groups * LANES), lambda b, h: (b, 0, h)),
        out_shape=jax.ShapeDtypeStruct((bsz, tq_len, C_HEADS * C_VDIM), BF16),
        compiler_params=_cparams("parallel", "parallel"))(q, kv, krp)


def _pad_rows(a, n):
    return jnp.pad(a, ((0, 0), (0, n - a.shape[1]), (0, 0)))


def _mlstm_layer(x, gain, state, conv0, prev, w, j, bsz, t_len):
    assert t_len >= 8 and t_len % HALO == 0
    conv0p = jnp.pad(conv0, ((0, 0), (8 - (A_CONV - 1), 0), (0, 0)))
    q, k, v, xc, z, gcol, grow, tail = _mlstm_front(x.reshape(bsz, t_len, -1), gain, conv0p, w, j)
    hn, c_st, n_st, m_st = _mlstm_rec(q, k, v, gcol, grow, w["g_head"][j], state, j, prev)
    flat = lambda a: a.reshape(bsz * t_len, A_INNER)
    return (flat(hn), flat(xc), flat(z), w["skip"][j]), (c_st, n_st, m_st), tail[:, 8 - (A_CONV - 1):]


def _sb_layer(x, gain, past, w, j, bsz, t_len, q_off):
    d = x.shape[1]
    q, k, kb, v, vb = _proj(x, gain, [(w["wqkv"], j)],
                            [(0, 0, d, BF16), (0, d, 2 * d, F32), (0, d, 2 * d, BF16),
                             (0, 2 * d, 3 * d, F32), (0, 2 * d, 3 * d, BF16)], name="sb_qkv")
    k3 = kb.reshape(bsz, t_len, d)
    v3 = vb.reshape(bsz, t_len, d)
    if past is not None:
        k3 = jnp.concatenate([past[0].reshape(bsz, -1, d).astype(BF16), k3], axis=1)
        v3 = jnp.concatenate([past[1].reshape(bsz, -1, d).astype(BF16), v3], axis=1)
    tk_pad = -(-k3.shape[1] // LANES) * LANES
    k3 = _pad_rows(k3, tk_pad)
    v3 = _pad_rows(v3, tk_pad)
    o = _sb_attn(q.reshape(bsz, t_len, d), k3, v3, w["u"], q_off)
    return o.reshape(bsz * t_len, d), (k.reshape(bsz, t_len, B_HEADS, B_DH), v.reshape(bsz, t_len, B_HEADS, B_DH))


def _rope_tables(t_len, q_off):
    half = C_ROPE // 2
    inv = ROPE_THETA ** (-jnp.arange(half, dtype=F32) / half)
    ang = (jnp.arange(t_len, dtype=F32) + q_off)[:, None] * inv[None, :]
    cos, sin = jnp.cos(ang), jnp.sin(ang)
    ones = jnp.ones((t_len, HALF), F32)
    zeros = jnp.zeros((t_len, HALF), F32)
    tail = LANES - HALF - C_ROPE
    cos_t = jnp.concatenate([ones, cos, cos, ones[:, :tail]], axis=1)
    sin_s = jnp.concatenate([zeros, -sin, sin, zeros[:, :tail]], axis=1)
    return cos_t, sin_s


def _mla_layer(x, gain, past, w, j, bsz, t_len, q_off):
    cos_t, sin_s = _rope_tables(t_len, q_off)
    outs = _mla_front(x, gain, w, j, cos_t, sin_s, t_len, fuse_kv=past is None)
    q, ckv, krp, krpb = outs[:4]
    ckv3 = ckv.reshape(bsz, t_len, C_KV_RANK)
    kr_new = krp.reshape(bsz, t_len, LANES)[:, :, HALF:HALF + C_ROPE]
    krpb3 = krpb.reshape(bsz, t_len, LANES)
    if past is None:
        assert t_len % LANES == 0
        n_valid = tk_pad = t_len
        kv, krp_all = outs[4], krpb3
    else:
        ckv_all = jnp.concatenate([past[0], ckv3], axis=1)
        kr_pad = jnp.pad(past[1], ((0, 0), (0, 0), (HALF, LANES - HALF - C_ROPE))).astype(BF16)
        n_valid = ckv_all.shape[1]
        tk_pad = -(-n_valid // LANES) * LANES
        krp_all = _pad_rows(jnp.concatenate([kr_pad, krpb3], axis=1), tk_pad)
        (kv,) = _proj(_pad_rows(ckv_all, tk_pad).reshape(bsz * tk_pad, C_KV_RANK), None, [(w["ukv"], j)],
                      [(0, 0, C_HEADS * LANES, BF16)], name="mla_kv_up")
    o = _mla_attn(q.reshape(bsz, t_len, C_HEADS * LANES), kv.reshape(bsz, tk_pad, C_HEADS * LANES),
                  krp_all, q_off, n_valid)
    return o.reshape(bsz * t_len, C_HEADS * C_VDIM), (ckv3, kr_new)


def _trunk(x3, q_off, st_a, past_b, past_c, wts):
    bsz, t_len, d = x3.shape
    x = x3.reshape(bsz * t_len, d)
    a_states, a_convs, new_b, new_c = None, [], [], []
    state = None
    if st_a is not None:
        m_rep = jnp.broadcast_to(st_a[2][..., None], st_a[2].shape + (LANES,))
        state = (st_a[0], st_a[1], m_rep)
    for i in range(DEPTH):
        j = i // 3
        if i % 3 == 0:
            conv0 = jnp.zeros((bsz, A_CONV - 1, A_INNER), F32) if st_a is None else st_a[3][j]
            mixed, a_states, conv_new = _mlstm_layer(x, wts["norm_mix"][i], state, conv0, a_states,
                                                     wts["a"], j, bsz, t_len)
            a_convs.append(conv_new)
            wo = wts["a"]["down"]
        elif i % 3 == 1:
            past = None if past_b is None else (past_b[0][j], past_b[1][j])
            mixed, kv = _sb_layer(x, wts["norm_mix"][i], past, wts["b"], j, bsz, t_len, q_off)
            new_b.append(kv)
            wo = wts["b"]["wo"]
        else:
            past = None if past_c is None else (past_c[0][j], past_c[1][j])
            mixed, lat = _mla_layer(x, wts["norm_mix"][i], past, wts["c"], j, bsz, t_len, q_off)
            new_c.append(lat)
            wo = wts["c"]["wo"]
        final = wts["norm_final"] if i == DEPTH - 1 else None
        x = _mixer_out_mlp(x, mixed, wo, j, wts["norm_mlp"][i], wts["ff1"], wts["ff2"], i, final_gain=final)
    stack = lambda items, idx: jnp.stack([it[idx] for it in items])
    return (x.reshape(bsz, t_len, d),
            a_states[0], a_states[1], a_states[2][..., 0], jnp.stack(a_convs),
            stack(new_b, 0), stack(new_b, 1), stack(new_c, 0), stack(new_c, 1))


def _prep_weights(norm_mix, norm_mlp, norm_final, a_w_up, a_conv_w, a_conv_b, a_w_q, a_w_k, a_w_v,
                  a_w_gate, a_b_i, a_b_f, a_g_head, a_skip, a_w_down, b_w_qkv, b_w_o, c_w_dq, c_g_q,
                  c_w_uq, c_w_dkv, c_g_kv, c_w_ukv, c_w_o, w_ff1, w_ff2):
    bf = lambda a: a.astype(BF16)
    n_a = a_w_up.shape[0]
    wg = a_w_gate.reshape(n_a, 3, A_HEADS, A_DH, 2 * A_HEADS)
    wg = jnp.pad(wg, ((0, 0),) * 4 + ((0, LANES - 2 * A_HEADS),))
    gb = jnp.pad(jnp.concatenate([a_b_i, a_b_f], axis=1), ((0, 0), (0, LANES - 2 * A_HEADS)))
    a = dict(up=bf(a_w_up), conv_w=a_conv_w, conv_b=a_conv_b.reshape(n_a, 1, A_INNER), wq=bf(a_w_q),
             wk=bf(a_w_k), wv=bf(a_w_v), wg=bf(wg), gb=gb.reshape(n_a, 1, LANES),
             g_head=a_g_head.reshape(n_a, 1, A_INNER), skip=a_skip.reshape(n_a, 1, A_INNER), down=bf(a_w_down))
    d = b_w_o.shape[1]
    qscale = jnp.where(jnp.arange(3 * d) < d, (B_DH ** -0.5) * LOG2E, 1.0).astype(F32)
    jj = lax.broadcasted_iota(jnp.int32, (2 * LANES, 2 * LANES), 0) % LANES
    ss = lax.broadcasted_iota(jnp.int32, (2 * LANES, 2 * LANES), 1)
    b = dict(wqkv=bf(b_w_qkv * qscale), wo=bf(b_w_o), u=-((ss >= LANES) | (jj > ss)).astype(BF16))
    n_c = c_w_dq.shape[0]
    uq = c_w_uq.reshape(n_c, -1, C_HEADS, C_NOPE + C_ROPE)
    uq = jnp.pad(uq, ((0, 0),) * 3 + ((0, LANES - C_NOPE - C_ROPE),)).reshape(n_c, -1, C_HEADS * LANES)
    dkv = jnp.concatenate([c_w_dkv[..., :C_KV_RANK], jnp.zeros(c_w_dkv.shape[:2] + (HALF,), F32),
                           c_w_dkv[..., C_KV_RANK:],
                           jnp.zeros(c_w_dkv.shape[:2] + (LANES - HALF - C_ROPE,), F32)], axis=-1)
    c = dict(dq=bf(c_w_dq), dkv=bf(dkv), g_q=c_g_q.reshape(n_c, 1, -1), g_kv=c_g_kv.reshape(n_c, 1, -1),
             uq=bf(uq), ukv=bf(c_w_ukv.reshape(n_c, C_KV_RANK, C_HEADS * LANES)), wo=bf(c_w_o))
    return dict(norm_mix=norm_mix, norm_mlp=norm_mlp, norm_final=norm_final, a=a, b=b, c=c,
                ff1=bf(w_ff1), ff2=bf(w_ff2))


def kernel(x_prompt, x_sample, state_mlstm_C, state_mlstm_n, state_mlstm_m, state_mlstm_conv, cache_sb_k, cache_sb_v, cache_mla_ckv, cache_mla_krope, norm_mix, norm_mlp, norm_final, a_w_up, a_conv_w, a_conv_b, a_w_q, a_w_k, a_w_v, a_w_gate, a_b_i, a_b_f, a_g_head, a_skip, a_w_down, b_w_qkv, b_w_o, c_w_dq, c_g_q, c_w_uq, c_w_dkv, c_g_kv, c_w_ukv, c_w_o, w_ff1, w_ff2):
    wts = _prep_weights(norm_mix, norm_mlp, norm_final, a_w_up, a_conv_w, a_conv_b, a_w_q, a_w_k, a_w_v,
                        a_w_gate, a_b_i, a_b_f, a_g_head, a_skip, a_w_down, b_w_qkv, b_w_o, c_w_dq, c_g_q,
                        c_w_uq, c_w_dkv, c_g_kv, c_w_ukv, c_w_o, w_ff1, w_ff2)
    outs_p = _trunk(x_prompt, 0, None, None, None, wts)
    past_len = cache_sb_k.shape[2]
    outs_s = _trunk(x_sample, past_len,
                    (state_mlstm_C, state_mlstm_n, state_mlstm_m, state_mlstm_conv),
                    (cache_sb_k, cache_sb_v), (cache_mla_ckv, cache_mla_krope), wts)
    y_p, rest_p = outs_p[0], outs_p[1:]
    y_s, rest_s = outs_s[0], outs_s[1:]
    return (y_p, y_s) + tuple(rest_p) + tuple(rest_s)
```
